```python
import math
import jax, jax.numpy as jnp
from jax import lax
import numpy as np

D_MODEL = 1024
BATCH = 16
SEQ = 4096
DEPTH = 4

N_A_LAYERS = DEPTH // 2
N_B_LAYERS = DEPTH - N_A_LAYERS

RET_HEADS = D_MODEL // 256
RET_DK = D_MODEL // RET_HEADS
RET_DV = 2 * D_MODEL // RET_HEADS
RET_CHUNK = 128
RET_IN = 2 * RET_HEADS * RET_DK + 2 * RET_HEADS * RET_DV

MLA_HEADS = 16
QK_NOPE = 128
QK_ROPE = 64
QK_HEAD = QK_NOPE + QK_ROPE
V_HEAD = 128
Q_LORA = 384
KV_LORA = 256
Q_BLOCK = 128
ATTN_SCALE = 1.0 / math.sqrt(QK_HEAD)

FFN_HIDDEN = ((8 * D_MODEL + 3 * 256 - 1) // (3 * 256)) * 256

ROPE_BASE = 10000.0
NORM_EPS = 1e-6

kernel_name = "yoco_retnet_mla_hybrid"


def rms_norm(x, g=None):
    xf = x.astype(jnp.float32)
    y = xf * lax.rsqrt(jnp.mean(xf * xf, axis=-1, keepdims=True) + NORM_EPS)
    if g is not None:
        y = y * g.astype(jnp.float32)
    return y.astype(x.dtype)


def rope(x, pos):
    d = x.shape[-1]
    inv_freq = ROPE_BASE ** (-jnp.arange(0, d, 2, dtype=jnp.float32) / d)
    ang = pos.astype(jnp.float32)[:, :, None, None] * inv_freq
    cos, sin = jnp.cos(ang), jnp.sin(ang)
    xf = x.astype(jnp.float32)
    x1, x2 = xf[..., : d // 2], xf[..., d // 2:]
    return jnp.concatenate([x1 * cos - x2 * sin, x2 * cos + x1 * sin], axis=-1).astype(x.dtype)


def swiglu_ffn(h, w_in, w_out):
    u = h @ w_in
    a, b = u[..., :FFN_HIDDEN], u[..., FFN_HIDDEN:]
    return (jax.nn.silu(a) * b) @ w_out


def chunk_retention(q, k, v):
    B, S, H, DK = q.shape
    DV = v.shape[-1]
    C = RET_CHUNK
    N = S // C
    log_gamma = jnp.log(1.0 - 2.0 ** (-5.0 - jnp.arange(H, dtype=jnp.float32)))
    idx = jnp.arange(C, dtype=jnp.float32)
    diff = idx[:, None] - idx[None, :]
    inner_decay = jnp.where(diff[None] >= 0, jnp.exp(jnp.maximum(diff, 0.0)[None] * log_gamma[:, None, None]), 0.0)
    q_decay = jnp.exp((idx + 1.0)[None, :] * log_gamma[:, None])
    k_decay = jnp.exp((C - 1.0 - idx)[None, :] * log_gamma[:, None])
    chunk_decay = jnp.exp(C * log_gamma)

    def to_chunks(t):
        d = t.shape[-1]
        return t.astype(jnp.float32).reshape(B, N, C, H, d).transpose(1, 0, 3, 2, 4)

    def step(state, inp):
        qc, kc, vc = inp
        inner = jnp.einsum('bhid,bhjd->bhij', qc, kc) * inner_decay
        o = jnp.einsum('bhij,bhje->bhie', inner, vc) + \
            jnp.einsum('bhid,bhde->bhie', qc * q_decay[..., None], state)
        state = state * chunk_decay[:, None, None] + \
            jnp.einsum('bhjd,bhje->bhde', kc * k_decay[..., None], vc)
        return state, o

    state0 = jnp.zeros((B, H, DK, DV), jnp.float32)
    _, o = lax.scan(step, state0, (to_chunks(q), to_chunks(k), to_chunks(v)))
    return o.transpose(1, 0, 3, 2, 4).reshape(B, S, H, DV).astype(v.dtype)


def retention_mixer(h, pos, w_in, w_out):
    B, S, _ = h.shape
    proj = h @ w_in
    o1 = RET_HEADS * RET_DK
    o2 = 2 * o1
    o3 = o2 + RET_HEADS * RET_DV
    q = rope(proj[..., :o1].reshape(B, S, RET_HEADS, RET_DK), pos)
    k = rope(proj[..., o1:o2].reshape(B, S, RET_HEADS, RET_DK), pos) * (RET_DK ** -0.5)
    v = proj[..., o2:o3].reshape(B, S, RET_HEADS, RET_DV)
    g = proj[..., o3:]
    o = rms_norm(chunk_retention(q, k, v))
    return (o.reshape(B, S, RET_HEADS * RET_DV) * jax.nn.silu(g)) @ w_out


def mla_shared_kv(h, pos, kv_norm_g, w_dkv, ckv_norm_g, w_uk, w_uv, k_norm_g):
    B, S, _ = h.shape
    ckr = rms_norm(h, kv_norm_g) @ w_dkv
    c = rms_norm(ckr[..., :KV_LORA], ckv_norm_g)
    k_rope = jnp.broadcast_to(ckr[..., None, KV_LORA:], (B, S, MLA_HEADS, QK_ROPE))
    k_nope = (c @ w_uk).reshape(B, S, MLA_HEADS, QK_NOPE)
    v = (c @ w_uv).reshape(B, S, MLA_HEADS, V_HEAD)
    k = rms_norm(jnp.concatenate([k_nope, k_rope], axis=-1), k_norm_g)
    k = jnp.concatenate([k[..., :QK_NOPE], rope(k[..., QK_NOPE:], pos)], axis=-1)
    return k, v


def causal_block_attention(q, k, v):
    S = q.shape[1]
    outs = []
    for i in range(S // Q_BLOCK):
        s0, e = i * Q_BLOCK, (i + 1) * Q_BLOCK
        sc = jnp.einsum('bqhd,bkhd->bhqk', q[:, s0:e], k[:, :e]).astype(jnp.float32) * ATTN_SCALE
        mask = jnp.arange(e)[None, :] <= (s0 + jnp.arange(Q_BLOCK))[:, None]
        p = jax.nn.softmax(jnp.where(mask, sc, -jnp.inf), axis=-1).astype(v.dtype)
        outs.append(jnp.einsum('bhqk,bkhe->bqhe', p, v[:, :e]))
    return jnp.concatenate(outs, axis=1)


def mla_mixer(h, pos, k, v, w_dq, q_lora_norm_g, w_uq, q_norm_g, w_o):
    B, S, _ = h.shape
    cq = rms_norm(h @ w_dq, q_lora_norm_g)
    q = rms_norm((cq @ w_uq).reshape(B, S, MLA_HEADS, QK_HEAD), q_norm_g)
    q = jnp.concatenate([q[..., :QK_NOPE], rope(q[..., QK_NOPE:], pos)], axis=-1)
    o = causal_block_attention(q, k, v)
    return o.reshape(B, S, MLA_HEADS * V_HEAD) @ w_o


def setup_inputs(seed: int = 0) -> dict:
    key = jax.random.key(seed)
    ks = jax.random.split(key, 20)

    def w(k, shape, fan_in, scale=1.0):
        return jax.random.normal(k, shape, jnp.float32) * (scale * fan_in ** -0.5)

    def gain(k, shape):
        return 1.0 + 0.02 * jax.random.normal(k, shape, jnp.float32)

    res_scale = (2.0 * DEPTH) ** -0.5
    x = jax.random.normal(ks[0], (BATCH, SEQ, D_MODEL), jnp.float32)
    offsets = jax.random.randint(ks[1], (BATCH, 1), 0, 4096, dtype=jnp.int32)
    positions = offsets + jnp.arange(SEQ, dtype=jnp.int32)[None, :]
    return {
        "x": x,
        "positions": positions,
        "attn_norm_g": gain(ks[2], (DEPTH, D_MODEL)),
        "ffn_norm_g": gain(ks[3], (DEPTH, D_MODEL)),
        "ffn_w_in": w(ks[4], (DEPTH, D_MODEL, 2 * FFN_HIDDEN), D_MODEL),
        "ffn_w_out": w(ks[5], (DEPTH, FFN_HIDDEN, D_MODEL), FFN_HIDDEN, res_scale),
        "ret_w_in": w(ks[6], (N_A_LAYERS, D_MODEL, RET_IN), D_MODEL),
        "ret_w_out": w(ks[7], (N_A_LAYERS, RET_HEADS * RET_DV, D_MODEL), RET_HEADS * RET_DV, res_scale),
        "kv_norm_g": gain(ks[8], (D_MODEL,)),
        "w_dkv": w(ks[9], (D_MODEL, KV_LORA + QK_ROPE), D_MODEL),
        "ckv_norm_g": gain(ks[10], (KV_LORA,)),
        "w_uk": w(ks[11], (KV_LORA, MLA_HEADS * QK_NOPE), KV_LORA),
        "w_uv": w(ks[12], (KV_LORA, MLA_HEADS * V_HEAD), KV_LORA),
        "k_norm_g": gain(ks[13], (QK_HEAD,)),
        "mla_w_dq": w(ks[14], (N_B_LAYERS, D_MODEL, Q_LORA), D_MODEL),
        "q_lora_norm_g": gain(ks[15], (N_B_LAYERS, Q_LORA)),
        "mla_w_uq": w(ks[16], (N_B_LAYERS, Q_LORA, MLA_HEADS * QK_HEAD), Q_LORA),
        "q_norm_g": gain(ks[17], (N_B_LAYERS, QK_HEAD)),
        "mla_w_o": w(ks[18], (N_B_LAYERS, MLA_HEADS * V_HEAD, D_MODEL), MLA_HEADS * V_HEAD, res_scale),
    }


def reference(x, positions, attn_norm_g, ffn_norm_g, ffn_w_in, ffn_w_out, ret_w_in, ret_w_out,
              kv_norm_g, w_dkv, ckv_norm_g, w_uk, w_uv, k_norm_g,
              mla_w_dq, q_lora_norm_g, mla_w_uq, q_norm_g, mla_w_o):
    h = x
    shared_k, shared_v = None, None
    for l in range(DEPTH):
        hn = rms_norm(h, attn_norm_g[l])
        if l < N_A_LAYERS:
            h = h + retention_mixer(hn, positions, ret_w_in[l], ret_w_out[l])
        else:
            j = l - N_A_LAYERS
            h = h + mla_mixer(hn, positions, shared_k, shared_v,
                              mla_w_dq[j], q_lora_norm_g[j], mla_w_uq[j], q_norm_g[j], mla_w_o[j])
        h = h + swiglu_ffn(rms_norm(h, ffn_norm_g[l]), ffn_w_in[l], ffn_w_out[l])
        if l == N_A_LAYERS - 1:
            shared_k, shared_v = mla_shared_kv(h, positions, kv_norm_g, w_dkv, ckv_norm_g,
                                               w_uk, w_uv, k_norm_g)
    return h
```

```python
import functools
import math

import jax
import jax.numpy as jnp
from jax import lax
from jax.experimental import pallas as pl
from jax.experimental.pallas import tpu as pltpu

F32 = jnp.float32
BF16 = jnp.bfloat16

D_MODEL = 1024
DEPTH = 4
N_RET_LAYERS = DEPTH // 2
RET_HEADS = 4
RET_DK = 256
RET_DV = 512
RET_QK = RET_HEADS * RET_DK
RET_V = RET_HEADS * RET_DV
RET_IN = 2 * RET_QK + 2 * RET_V
MLA_HEADS = 16
QK_NOPE = 128
QK_ROPE = 64
QK_HEAD = QK_NOPE + QK_ROPE
V_HEAD = 128
Q_LORA = 384
KV_LORA = 256
FFN_HIDDEN = 2816
ROPE_BASE = 10000.0
NORM_EPS = 1e-6
ATTN_SCALE = 1.0 / math.sqrt(QK_HEAD)

LANES = 128
MXU_DIM = 256
HEAD_PAD = 2 * LANES
VMEM_LIMIT_BYTES = 56 * 1024 * 1024

ROW_TILE = 512
FFN_CHUNK = MXU_DIM
RET_CHUNK = 256
RET_ROWS = 1024
ATTN_TQ = 512
ATTN_TK = 512


def _cparams(*sem):
    return pltpu.CompilerParams(dimension_semantics=sem, vmem_limit_bytes=VMEM_LIMIT_BYTES)


def _resident(shape):
    zeros = (0,) * len(shape)
    return pl.BlockSpec(shape, lambda *_: zeros, pipeline_mode=pl.Buffered(1))


def _rows(width, tile=ROW_TILE):
    return pl.BlockSpec((tile, width), lambda i: (i, 0))


def _rms(x):
    return x * lax.rsqrt(jnp.mean(x * x, axis=-1, keepdims=True) + NORM_EPS)


def _dot(a, b):
    return jnp.dot(a, b, preferred_element_type=F32)


def _dot_nt(a, b):
    return lax.dot_general(a, b, (((1,), (1,)), ((), ())), preferred_element_type=F32)


def _dot_tn(a, b):
    return lax.dot_general(a, b, (((0,), (0,)), ((), ())), preferred_element_type=F32)


def _rope_table_kernel(pos_ref, fr_ref, fm_ref, sga_ref, sgb_ref,
                       cr_ref, sr_ref, cm_ref, sa_ref, sb_ref):
    p = pos_ref[...].astype(F32)
    ar = p * fr_ref[...]
    cr_ref[...] = jnp.cos(ar)
    sr_ref[...] = jnp.sin(ar)
    am = p * fm_ref[...]
    sm = jnp.sin(am)
    cm_ref[...] = jnp.cos(am)
    sa_ref[...] = sm * sga_ref[...]
    sb_ref[...] = sm * sgb_ref[...]


def _rope_tables(positions):
    t = positions.size
    pos = positions.reshape(t, 1)
    half_r = RET_DK // 2
    fr = ROPE_BASE ** (-jnp.arange(0, RET_DK, 2, dtype=F32) / RET_DK)
    fm32 = ROPE_BASE ** (-jnp.arange(0, QK_ROPE, 2, dtype=F32) / QK_ROPE)
    half_m = QK_ROPE // 2
    zeros = jnp.zeros((LANES - QK_ROPE,), F32)
    fm = jnp.concatenate([fm32, fm32, zeros])
    sga = jnp.concatenate([-jnp.ones((half_m,), F32), jnp.zeros((LANES - half_m,), F32)])
    sgb = jnp.concatenate([jnp.zeros((half_m,), F32), jnp.ones((half_m,), F32), zeros])
    row = lambda a: a.reshape(1, LANES)
    assert half_r == LANES
    tab = jax.ShapeDtypeStruct((t, LANES), F32)
    return pl.pallas_call(
        _rope_table_kernel,
        grid=(t // ROW_TILE,),
        in_specs=[_rows(1)] + [_resident((1, LANES))] * 4,
        out_specs=[_rows(LANES)] * 5,
        out_shape=[tab] * 5,
        compiler_params=_cparams("parallel"),
        name="rope_tables",
    )(pos, row(fr), row(fm), row(sga), row(sgb))


def _rope_mla(x, cm, sa, sb):
    return x * cm + pltpu.roll(x, 96, 1) * sa + pltpu.roll(x, 32, 1) * sb


def _ret_proj_kernel(h_ref, g_ref, w_ref, c_ref, s_ref, q_ref, k_ref, v_ref, gs_ref):
    hn = (_rms(h_ref[...]) * g_ref[...]).astype(BF16)
    c = c_ref[...]
    s = s_ref[...]
    half = RET_DK // 2
    for dst, base, scale in ((q_ref, 0, 1.0), (k_ref, RET_QK, RET_DK ** -0.5)):
        for hd in range(RET_HEADS):
            lo = hd * RET_DK
            p = _dot(hn, w_ref[:, base + lo:base + lo + RET_DK])
            x1, x2 = p[:, :half], p[:, half:]
            dst[:, lo:lo + half] = ((x1 * c - x2 * s) * scale).astype(BF16)
            dst[:, lo + half:lo + RET_DK] = ((x2 * c + x1 * s) * scale).astype(BF16)
    for hd in range(RET_HEADS):
        lo = hd * RET_DV
        v_ref[:, lo:lo + RET_DV] = _dot(hn, w_ref[:, 2 * RET_QK + lo:2 * RET_QK + lo + RET_DV]).astype(BF16)
    for hd in range(RET_HEADS):
        lo = hd * RET_DV
        gate = _dot(hn, w_ref[:, 2 * RET_QK + RET_V + lo:2 * RET_QK + RET_V + lo + RET_DV])
        gs_ref[:, lo:lo + RET_DV] = (gate * jax.nn.sigmoid(gate)).astype(BF16)


def _ret_proj(h, g, w, cr, sr):
    t = h.shape[0]
    return pl.pallas_call(
        _ret_proj_kernel,
        grid=(t // ROW_TILE,),
        in_specs=[_rows(D_MODEL), _resident((1, D_MODEL)), _resident((D_MODEL, RET_IN)),
                  _rows(LANES), _rows(LANES)],
        out_specs=[_rows(RET_QK), _rows(RET_QK), _rows(RET_V), _rows(RET_V)],
        out_shape=[jax.ShapeDtypeStruct((t, RET_QK), BF16), jax.ShapeDtypeStruct((t, RET_QK), BF16),
                   jax.ShapeDtypeStruct((t, RET_V), BF16), jax.ShapeDtypeStruct((t, RET_V), BF16)],
        compiler_params=_cparams("parallel"),
        name="ret_proj",
    )(h, g, w, cr, sr)


def _retention_kernel(lg_ref, q_ref, k_ref, v_ref, gs_ref, o_ref, state_ref):
    c = RET_CHUNK

    @pl.when(pl.program_id(2) == 0)
    def _():
        state_ref[...] = jnp.zeros_like(state_ref)

    lg = lg_ref[pl.program_id(1)]
    ii = lax.broadcasted_iota(jnp.int32, (c, c), 0)
    jj = lax.broadcasted_iota(jnp.int32, (c, c), 1)
    diff = (ii - jj).astype(F32)
    inner_decay = jnp.where(diff >= 0, jnp.exp(jnp.maximum(diff, 0.0) * lg), 0.0)
    ri = lax.broadcasted_iota(jnp.int32, (c, 1), 0).astype(F32)
    q_decay = jnp.exp((ri + 1.0) * lg)
    k_decay = jnp.exp((c - 1.0 - ri) * lg)
    chunk_decay = jnp.exp(jnp.full((1, RET_DV), float(c), F32) * lg)

    for ci in range(RET_ROWS // c):
        rows = slice(ci * c, (ci + 1) * c)
        qc = q_ref[rows, :]
        kc = k_ref[rows, :]
        vc = v_ref[rows, :]
        st = state_ref[...]
        inner = (_dot_nt(qc, kc) * inner_decay).astype(BF16)
        o = _dot(inner, vc) + _dot(qc, st.astype(BF16)) * q_decay
        kd = (kc.astype(F32) * k_decay).astype(BF16)
        state_ref[...] = st * chunk_decay + _dot_tn(kd, vc)
        o_ref[rows, :] = (_rms(o) * gs_ref[rows, :].astype(F32)).astype(BF16)


def _retention(log_gamma, q, k, v, gs, batch, seq):
    q3 = q.reshape(batch, seq, RET_QK)
    k3 = k.reshape(batch, seq, RET_QK)
    v3 = v.reshape(batch, seq, RET_V)
    g3 = gs.reshape(batch, seq, RET_V)
    qk_spec = pl.BlockSpec((None, RET_ROWS, RET_DK), lambda b, h, s: (b, s, h))
    v_spec = pl.BlockSpec((None, RET_ROWS, RET_DV), lambda b, h, s: (b, s, h))
    out = pl.pallas_call(
        _retention_kernel,
        grid=(batch, RET_HEADS, seq // RET_ROWS),
        in_specs=[pl.BlockSpec(memory_space=pltpu.SMEM), qk_spec, qk_spec, v_spec, v_spec],
        out_specs=v_spec,
        out_shape=jax.ShapeDtypeStruct((batch, seq, RET_V), BF16),
        scratch_shapes=[pltpu.VMEM((RET_DK, RET_DV), F32)],
        compiler_params=_cparams("parallel", "parallel", "arbitrary"),
        name="retention",
    )(log_gamma, q3, k3, v3, g3)
    return out.reshape(batch * seq, RET_V)


def _out_proj_kernel(y_ref, w_ref, h_ref, o_ref):
    o_ref[...] = h_ref[...] + _dot(y_ref[...], w_ref[...])


def _out_proj(y, w, h):
    t, kdim = y.shape
    return pl.pallas_call(
        _out_proj_kernel,
        grid=(t // ROW_TILE,),
        in_specs=[_rows(kdim), _resident((kdim, D_MODEL)), _rows(D_MODEL)],
        out_specs=_rows(D_MODEL),
        out_shape=jax.ShapeDtypeStruct((t, D_MODEL), F32),
        compiler_params=_cparams("parallel"),
        name="out_proj",
    )(y, w, h)


def _ffn_kernel(h_ref, g_ref, wi_ref, wo_ref, o_ref, act_ref):
    x = h_ref[...]
    hn = (_rms(x) * g_ref[...]).astype(BF16)
    for f in range(FFN_HIDDEN // FFN_CHUNK):
        lo = f * FFN_CHUNK
        a = _dot(hn, wi_ref[:, lo:lo + FFN_CHUNK])
        b = _dot(hn, wi_ref[:, FFN_HIDDEN + lo:FFN_HIDDEN + lo + FFN_CHUNK])
        act_ref[:, lo:lo + FFN_CHUNK] = (a * jax.nn.sigmoid(a) * b).astype(BF16)
    o_ref[...] = x + _dot(act_ref[...], wo_ref[...])


def _ffn(h, g, w_in, w_out):
    t = h.shape[0]
    return pl.pallas_call(
        _ffn_kernel,
        grid=(t // ROW_TILE,),
        in_specs=[_rows(D_MODEL), _resident((1, D_MODEL)), _resident((D_MODEL, 2 * FFN_HIDDEN)),
                  _resident((FFN_HIDDEN, D_MODEL))],
        out_specs=_rows(D_MODEL),
        out_shape=jax.ShapeDtypeStruct((t, D_MODEL), F32),
        scratch_shapes=[pltpu.VMEM((ROW_TILE, FFN_HIDDEN), BF16)],
        compiler_params=_cparams("parallel"),
        name="ffn",
    )(h, g, w_in, w_out)


def _kv_kernel(h_ref, g_ref, wd_ref, cg_ref, wuk_ref, wuv_ref, gn_ref, gr_ref,
               cm_ref, sa_ref, sb_ref, k_ref, v_ref):
    hn = (_rms(h_ref[...]) * g_ref[...]).astype(BF16)
    ckr = _dot(hn, wd_ref[...])
    c = (_rms(ckr[:, :KV_LORA]) * cg_ref[...]).astype(BF16)
    kr = ckr[:, KV_LORA:]
    kr_ss = jnp.sum(kr * kr, axis=-1, keepdims=True)
    kr_rot = _rope_mla(kr * gr_ref[...], cm_ref[...], sa_ref[...], sb_ref[...])
    gn = gn_ref[...]
    for hd in range(MLA_HEADS):
        kn = _dot(c, wuk_ref[:, hd * QK_NOPE:(hd + 1) * QK_NOPE])
        r = lax.rsqrt((jnp.sum(kn * kn, axis=-1, keepdims=True) + kr_ss) * (1.0 / QK_HEAD) + NORM_EPS)
        k_ref[:, hd * HEAD_PAD:hd * HEAD_PAD + LANES] = (kn * r * gn).astype(BF16)
        k_ref[:, hd * HEAD_PAD + LANES:(hd + 1) * HEAD_PAD] = (kr_rot * r).astype(BF16)
    v_ref[...] = _dot(c, wuv_ref[...]).astype(BF16)


def _shared_kv(h, g, wd, cg, wuk, wuv, gn, gr, cm, sa, sb):
    t = h.shape[0]
    return pl.pallas_call(
        _kv_kernel,
        grid=(t // ROW_TILE,),
        in_specs=[_rows(D_MODEL), _resident((1, D_MODEL)), _resident((D_MODEL, KV_LORA + LANES)),
                  _resident((1, KV_LORA)), _resident((KV_LORA, MLA_HEADS * QK_NOPE)),
                  _resident((KV_LORA, MLA_HEADS * V_HEAD)), _resident((1, LANES)), _resident((1, LANES)),
                  _rows(LANES), _rows(LANES), _rows(LANES)],
        out_specs=[_rows(MLA_HEADS * HEAD_PAD), _rows(MLA_HEADS * V_HEAD)],
        out_shape=[jax.ShapeDtypeStruct((t, MLA_HEADS * HEAD_PAD), BF16),
                   jax.ShapeDtypeStruct((t, MLA_HEADS * V_HEAD), BF16)],
        compiler_params=_cparams("parallel"),
        name="shared_kv",
    )(h, g, wd, cg, wuk, wuv, gn, gr, cm, sa, sb)


def _q_proj_kernel(h_ref, g_ref, wd_ref, lg_ref, wu_ref, gn_ref, gr_ref,
                   cm_ref, sa_ref, sb_ref, q_ref):
    hn = (_rms(h_ref[...]) * g_ref[...]).astype(BF16)
    cq = (_rms(_dot(hn, wd_ref[...])) * lg_ref[...]).astype(BF16)
    cm = cm_ref[...]
    sa = sa_ref[...]
    sb = sb_ref[...]
    gn = gn_ref[...]
    gr = gr_ref[...]
    for hd in range(MLA_HEADS):
        qh = _dot(cq, wu_ref[:, hd * HEAD_PAD:(hd + 1) * HEAD_PAD])
        qn, qr = qh[:, :LANES], qh[:, LANES:]
        ss = jnp.sum(qn * qn, axis=-1, keepdims=True) + jnp.sum(qr * qr, axis=-1, keepdims=True)
        r = lax.rsqrt(ss * (1.0 / QK_HEAD) + NORM_EPS) * ATTN_SCALE
        q_ref[:, hd * HEAD_PAD:hd * HEAD_PAD + LANES] = (qn * r * gn).astype(BF16)
        q_ref[:, hd * HEAD_PAD + LANES:(hd + 1) * HEAD_PAD] = _rope_mla(qr * gr * r, cm, sa, sb).astype(BF16)


def _q_proj(h, g, wd, lg, wu, gn, gr, cm, sa, sb):
    t = h.shape[0]
    return pl.pallas_call(
        _q_proj_kernel,
        grid=(t // ROW_TILE,),
        in_specs=[_rows(D_MODEL), _resident((1, D_MODEL)), _resident((D_MODEL, Q_LORA)),
                  _resident((1, Q_LORA)), _resident((Q_LORA, MLA_HEADS * HEAD_PAD)),
                  _resident((1, LANES)), _resident((1, LANES)),
                  _rows(LANES), _rows(LANES), _rows(LANES)],
        out_specs=_rows(MLA_HEADS * HEAD_PAD),
        out_shape=jax.ShapeDtypeStruct((t, MLA_HEADS * HEAD_PAD), BF16),
        compiler_params=_cparams("parallel"),
        name="q_proj",
    )(h, g, wd, lg, wu, gn, gr, cm, sa, sb)


def _attn_kernel(q_ref, k_ref, v_ref, o_ref, m_ref, l_ref, acc_ref):
    qi = pl.program_id(2)
    q = q_ref[...]
    m_ref[...] = jnp.full_like(m_ref, -jnp.inf)
    l_ref[...] = jnp.zeros_like(l_ref)
    acc_ref[...] = jnp.zeros_like(acc_ref)

    def block(kb, diagonal):
        start = pl.multiple_of(kb * ATTN_TK, ATTN_TK)
        s = _dot_nt(q, k_ref[pl.ds(start, ATTN_TK), :])
        if diagonal:
            row = lax.broadcasted_iota(jnp.int32, s.shape, 0)
            col = lax.broadcasted_iota(jnp.int32, s.shape, 1)
            s = jnp.where(col <= row, s, -jnp.inf)
        m_prev = m_ref[...]
        m_new = jnp.maximum(m_prev, jnp.max(s, axis=-1, keepdims=True))
        alpha = jnp.exp(m_prev - m_new)
        p = jnp.exp(s - m_new)
        l_ref[...] = alpha * l_ref[...] + jnp.sum(p, axis=-1, keepdims=True)
        acc_ref[...] = alpha * acc_ref[...] + _dot(p.astype(BF16), v_ref[pl.ds(start, ATTN_TK), :])
        m_ref[...] = m_new

    def body(kb, carry):
        block(kb, False)
        return carry

    lax.fori_loop(0, qi, body, 0)
    block(qi, True)
    o_ref[...] = (acc_ref[...] / l_ref[...]).astype(BF16)


def _attention(q, k, v, batch, seq):
    assert ATTN_TQ == ATTN_TK
    q3 = q.reshape(batch, seq, MLA_HEADS * HEAD_PAD)
    k3 = k.reshape(batch, seq, MLA_HEADS * HEAD_PAD)
    v3 = v.reshape(batch, seq, MLA_HEADS * V_HEAD)
    out = pl.pallas_call(
        _attn_kernel,
        grid=(batch, MLA_HEADS, seq // ATTN_TQ),
        in_specs=[pl.BlockSpec((None, ATTN_TQ, HEAD_PAD), lambda b, h, i: (b, i, h)),
                  pl.BlockSpec((None, seq, HEAD_PAD), lambda b, h, i: (b, 0, h)),
                  pl.BlockSpec((None, seq, V_HEAD), lambda b, h, i: (b, 0, h))],
        out_specs=pl.BlockSpec((None, ATTN_TQ, V_HEAD), lambda b, h, i: (b, i, h)),
        out_shape=jax.ShapeDtypeStruct((batch, seq, MLA_HEADS * V_HEAD), BF16),
        scratch_shapes=[pltpu.VMEM((ATTN_TQ, 1), F32), pltpu.VMEM((ATTN_TQ, 1), F32),
                        pltpu.VMEM((ATTN_TQ, V_HEAD), F32)],
        compiler_params=_cparams("parallel", "parallel", "parallel"),
        name="attention",
    )(q3, k3, v3)
    return out.reshape(batch * seq, MLA_HEADS * V_HEAD)


def _pad_lanes(a, width):
    return jnp.pad(a, [(0, 0)] * (a.ndim - 1) + [(0, width - a.shape[-1])])


def kernel(x, positions, attn_norm_g, ffn_norm_g, ffn_w_in, ffn_w_out, ret_w_in, ret_w_out,
           kv_norm_g, w_dkv, ckv_norm_g, w_uk, w_uv, k_norm_g,
           mla_w_dq, q_lora_norm_g, mla_w_uq, q_norm_g, mla_w_o):
    batch, seq, _ = x.shape
    t = batch * seq
    row = lambda a: a.reshape(1, -1)

    cr, sr, cm, sa, sb = _rope_tables(positions)
    log_gamma = jnp.log(1.0 - 2.0 ** (-5.0 - jnp.arange(RET_HEADS, dtype=F32)))

    h = x.reshape(t, D_MODEL)
    for l in range(DEPTH):
        if l < N_RET_LAYERS:
            q, k, v, gs = _ret_proj(h, row(attn_norm_g[l]), ret_w_in[l].astype(BF16), cr, sr)
            y = _retention(log_gamma, q, k, v, gs, batch, seq)
            h = _out_proj(y, ret_w_out[l].astype(BF16), h)
        else:
            j = l - N_RET_LAYERS
            wu = _pad_lanes(mla_w_uq[j].reshape(Q_LORA, MLA_HEADS, QK_HEAD), HEAD_PAD)
            wu = wu.reshape(Q_LORA, MLA_HEADS * HEAD_PAD).astype(BF16)
            q = _q_proj(h, row(attn_norm_g[l]), mla_w_dq[j].astype(BF16), row(q_lora_norm_g[j]), wu,
                        row(q_norm_g[j][:QK_NOPE]), row(_pad_lanes(q_norm_g[j][QK_NOPE:], LANES)),
                        cm, sa, sb)
            y = _attention(q, shared_k, shared_v, batch, seq)
            h = _out_proj(y, mla_w_o[j].astype(BF16), h)
        h = _ffn(h, row(ffn_norm_g[l]), ffn_w_in[l].astype(BF16), ffn_w_out[l].astype(BF16))
        if l == N_RET_LAYERS - 1:
            shared_k, shared_v = _shared_kv(
                h, row(kv_norm_g), _pad_lanes(w_dkv, KV_LORA + LANES).astype(BF16), row(ckv_norm_g),
                w_uk.astype(BF16), w_uv.astype(BF16),
                row(k_norm_g[:QK_NOPE]), row(_pad_lanes(k_norm_g[QK_NOPE:], LANES)), cm, sa, sb)
    return h.reshape(batch, seq, D_MODEL)
```

```python
import math

import jax
import jax.numpy as jnp
from jax import lax
from jax.experimental import pallas as pl
from jax.experimental.pallas import tpu as pltpu

F32 = jnp.float32
BF16 = jnp.bfloat16

D_MODEL = 1024
DEPTH = 4
N_RET_LAYERS = DEPTH // 2
RET_HEADS = 4
RET_DK = 256
RET_DV = 512
RET_QK = RET_HEADS * RET_DK
RET_V = RET_HEADS * RET_DV
RET_IN = 2 * RET_QK + 2 * RET_V
MLA_HEADS = 16
QK_NOPE = 128
QK_ROPE = 64
QK_HEAD = QK_NOPE + QK_ROPE
ROPE_HALF = QK_ROPE // 2
V_HEAD = 128
Q_LORA = 384
KV_LORA = 256
FFN_HIDDEN = 2816
ROPE_BASE = 10000.0
NORM_EPS = 1e-6
ATTN_SCALE = 1.0 / math.sqrt(QK_HEAD)

LANES = 128
MXU_DIM = 256
HEAD_PAD = 2 * LANES
VMEM_LIMIT_BYTES = 56 * 1024 * 1024

ROW_TILE = 512
FFN_CHUNK = MXU_DIM
RET_CHUNK = 256
RET_ROWS = 1024
ATTN_TQ = 512


def _cparams(*sem):
    return pltpu.CompilerParams(dimension_semantics=sem, vmem_limit_bytes=VMEM_LIMIT_BYTES)


def _resident(shape):
    zeros = (0,) * len(shape)
    return pl.BlockSpec(shape, lambda *_: zeros, pipeline_mode=pl.Buffered(1))


def _rows(width, tile=ROW_TILE):
    return pl.BlockSpec((tile, width), lambda i: (i, 0))


def _rms(x):
    return x * lax.rsqrt(jnp.mean(x * x, axis=-1, keepdims=True) + NORM_EPS)


def _dot(a, b):
    return jnp.dot(a, b, preferred_element_type=F32)


def _dot_nt(a, b):
    return lax.dot_general(a, b, (((1,), (1,)), ((), ())), preferred_element_type=F32)


def _dot_tn(a, b):
    return lax.dot_general(a, b, (((0,), (0,)), ((), ())), preferred_element_type=F32)


def _rope_table_kernel(pos_ref, fr_ref, fm_ref, sg_ref, cr_ref, sr_ref, cm_ref, sm_ref):
    p = pos_ref[...].astype(F32)
    ar = p * fr_ref[...]
    cr_ref[...] = jnp.cos(ar)
    sr_ref[...] = jnp.sin(ar)
    am = p * fm_ref[...]
    cm_ref[...] = jnp.cos(am)
    sm_ref[...] = jnp.sin(am) * sg_ref[...]


def _rope_tables(positions):
    t = positions.size
    pos = positions.reshape(t, 1)
    fr = ROPE_BASE ** (-jnp.arange(0, RET_DK, 2, dtype=F32) / RET_DK)
    fm32 = ROPE_BASE ** (-jnp.arange(0, QK_ROPE, 2, dtype=F32) / QK_ROPE)
    fm = jnp.tile(fm32, LANES // ROPE_HALF)
    sg = jnp.tile(jnp.concatenate([-jnp.ones((ROPE_HALF,), F32), jnp.ones((ROPE_HALF,), F32)]),
                  LANES // QK_ROPE)
    row = lambda a: a.reshape(1, LANES)
    assert RET_DK // 2 == LANES
    tab = jax.ShapeDtypeStruct((t, LANES), F32)
    return pl.pallas_call(
        _rope_table_kernel,
        grid=(t // ROW_TILE,),
        in_specs=[_rows(1)] + [_resident((1, LANES))] * 3,
        out_specs=[_rows(LANES)] * 4,
        out_shape=[tab] * 4,
        compiler_params=_cparams("parallel"),
        name="rope_tables",
    )(pos, row(fr), row(fm), row(sg))


def _lane_mask(lo, hi):
    lane = lax.broadcasted_iota(jnp.int32, (1, LANES), 1)
    return jnp.logical_and(lane >= lo, lane < hi)


def _ret_proj_kernel(lg_ref, h_ref, g_ref, w_ref, c_ref, s_ref, q_ref, k_ref, v_ref, gs_ref):
    hn = (_rms(h_ref[...]) * g_ref[...]).astype(BF16)
    c = c_ref[...]
    s = s_ref[...]
    half = RET_DK // 2
    in_chunk = jnp.bitwise_and(lax.broadcasted_iota(jnp.int32, (ROW_TILE, 1), 0), RET_CHUNK - 1)
    back = (RET_CHUNK - 1 - in_chunk).astype(F32)
    for dst, base in ((q_ref, 0), (k_ref, RET_QK)):
        for hd in range(RET_HEADS):
            lo = hd * RET_DK
            p = _dot(hn, w_ref[:, base + lo:base + lo + RET_DK])
            x1, x2 = p[:, :half], p[:, half:]
            scale = 1.0 if dst is q_ref else jnp.exp(back * lg_ref[hd]) * RET_DK ** -0.5
            dst[:, lo:lo + half] = ((x1 * c - x2 * s) * scale).astype(BF16)
            dst[:, lo + half:lo + RET_DK] = ((x2 * c + x1 * s) * scale).astype(BF16)
    for hd in range(RET_HEADS):
        lo = hd * RET_DV
        v_ref[:, lo:lo + RET_DV] = _dot(hn, w_ref[:, 2 * RET_QK + lo:2 * RET_QK + lo + RET_DV]).astype(BF16)
    for hd in range(RET_HEADS):
        lo = hd * RET_DV
        gate = _dot(hn, w_ref[:, 2 * RET_QK + RET_V + lo:2 * RET_QK + RET_V + lo + RET_DV])
        gs_ref[:, lo:lo + RET_DV] = (gate * jax.nn.sigmoid(gate)).astype(BF16)


def _ret_proj(log_gamma, h, g, w, cr, sr):
    t = h.shape[0]
    assert ROW_TILE % RET_CHUNK == 0
    return pl.pallas_call(
        _ret_proj_kernel,
        grid=(t // ROW_TILE,),
        in_specs=[pl.BlockSpec(memory_space=pltpu.SMEM),
                  _rows(D_MODEL), _resident((1, D_MODEL)), _resident((D_MODEL, RET_IN)),
                  _rows(LANES), _rows(LANES)],
        out_specs=[_rows(RET_QK), _rows(RET_QK), _rows(RET_V), _rows(RET_V)],
        out_shape=[jax.ShapeDtypeStruct((t, RET_QK), BF16), jax.ShapeDtypeStruct((t, RET_QK), BF16),
                   jax.ShapeDtypeStruct((t, RET_V), BF16), jax.ShapeDtypeStruct((t, RET_V), BF16)],
        compiler_params=_cparams("parallel"),
        name="ret_proj",
    )(log_gamma, h, g, w, cr, sr)


def _retention_kernel(lg_ref, q_ref, k_ref, v_ref, gs_ref, o_ref, state_ref):
    c = RET_CHUNK

    @pl.when(pl.program_id(1) == 0)
    def _():
        state_ref[...] = jnp.zeros_like(state_ref)

    ii = lax.broadcasted_iota(jnp.int32, (c, c), 0)
    jj = lax.broadcasted_iota(jnp.int32, (c, c), 1)
    lower = jj <= ii
    back = (c - 1 - lax.broadcasted_iota(jnp.int32, (c, 1), 0)).astype(F32)

    consts = []
    for hd in range(RET_HEADS):
        lg = lg_ref[hd]
        inv_rho = jnp.exp(back * lg)
        consts.append((NORM_EPS * inv_rho * inv_rho,
                       jnp.exp(jnp.full((1, RET_DV), float(c), F32) * lg)))

    for ci in range(RET_ROWS // c):
        rows = slice(ci * c, (ci + 1) * c)
        for hd in range(RET_HEADS):
            eps_rows, chunk_decay = consts[hd]
            qk_cols = slice(hd * RET_DK, (hd + 1) * RET_DK)
            v_cols = slice(hd * RET_DV, (hd + 1) * RET_DV)
            qc = q_ref[rows, qk_cols]
            kc = k_ref[rows, qk_cols]
            vc = v_ref[rows, v_cols]
            st = state_ref[hd] * chunk_decay
            p = jnp.where(lower, _dot_nt(qc, kc), 0.0).astype(BF16)
            o = _dot(qc, st.astype(BF16)) + _dot(p, vc)
            state_ref[hd] = st + _dot_tn(kc, vc)
            y = o * lax.rsqrt(jnp.mean(o * o, axis=-1, keepdims=True) + eps_rows)
            o_ref[rows, v_cols] = y.astype(BF16) * gs_ref[rows, v_cols]


def _retention(log_gamma, q, k, v, gs, batch, seq):
    q3 = q.reshape(batch, seq, RET_QK)
    k3 = k.reshape(batch, seq, RET_QK)
    v3 = v.reshape(batch, seq, RET_V)
    g3 = gs.reshape(batch, seq, RET_V)
    qk_spec = pl.BlockSpec((None, RET_ROWS, RET_QK), lambda b, s: (b, s, 0))
    v_spec = pl.BlockSpec((None, RET_ROWS, RET_V), lambda b, s: (b, s, 0))
    out = pl.pallas_call(
        _retention_kernel,
        grid=(batch, seq // RET_ROWS),
        in_specs=[pl.BlockSpec(memory_space=pltpu.SMEM), qk_spec, qk_spec, v_spec, v_spec],
        out_specs=v_spec,
        out_shape=jax.ShapeDtypeStruct((batch, seq, RET_V), BF16),
        scratch_shapes=[pltpu.VMEM((RET_HEADS, RET_DK, RET_DV), F32)],
        compiler_params=_cparams("parallel", "arbitrary"),
        name="retention",
    )(log_gamma, q3, k3, v3, g3)
    return out.reshape(batch * seq, RET_V)


def _mix_ffn_kernel(y_ref, wm_ref, h_ref, g_ref, wi_ref, wo_ref, o_ref, act_ref):
    x = h_ref[...] + _dot(y_ref[...], wm_ref[...])
    hn = (_rms(x) * g_ref[...]).astype(BF16)
    for f in range(FFN_HIDDEN // FFN_CHUNK):
        lo = f * FFN_CHUNK
        a = _dot(hn, wi_ref[:, lo:lo + FFN_CHUNK])
        b = _dot(hn, wi_ref[:, FFN_HIDDEN + lo:FFN_HIDDEN + lo + FFN_CHUNK])
        act_ref[:, lo:lo + FFN_CHUNK] = (a * jax.nn.sigmoid(a) * b).astype(BF16)
    o_ref[...] = x + _dot(act_ref[...], wo_ref[...])


def _mix_ffn(y, w_mix, h, g, w_in, w_out):
    t, kdim = y.shape
    return pl.pallas_call(
        _mix_ffn_kernel,
        grid=(t // ROW_TILE,),
        in_specs=[_rows(kdim), _resident((kdim, D_MODEL)), _rows(D_MODEL), _resident((1, D_MODEL)),
                  _resident((D_MODEL, 2 * FFN_HIDDEN)), _resident((FFN_HIDDEN, D_MODEL))],
        out_specs=_rows(D_MODEL),
        out_shape=jax.ShapeDtypeStruct((t, D_MODEL), F32),
        scratch_shapes=[pltpu.VMEM((ROW_TILE, FFN_HIDDEN), BF16)],
        compiler_params=_cparams("parallel"),
        name="mix_ffn",
    )(y, w_mix, h, g, w_in, w_out)


def _kv_kernel(h_ref, g_ref, wd_ref, cg_ref, wuk_ref, wuv_ref, gn_ref, gr_ref,
               cm_ref, sm_ref, k_ref, v_ref):
    hn = (_rms(h_ref[...]) * g_ref[...]).astype(BF16)
    ckr = _dot(hn, wd_ref[...])
    c = (_rms(ckr[:, :KV_LORA]) * cg_ref[...]).astype(BF16)
    kr = ckr[:, KV_LORA:]
    kr_ss = jnp.sum(kr * kr, axis=-1, keepdims=True)
    krg = kr * gr_ref[...]
    sm = sm_ref[...]
    sa = jnp.where(_lane_mask(0, ROPE_HALF), sm, 0.0)
    sb = jnp.where(_lane_mask(ROPE_HALF, QK_ROPE), sm, 0.0)
    rot_a = krg * cm_ref[...] + pltpu.roll(krg, LANES - ROPE_HALF, 1) * sa + pltpu.roll(krg, ROPE_HALF, 1) * sb
    rot_b = pltpu.roll(rot_a, QK_ROPE, 1)
    gn = gn_ref[...]
    for hd in range(MLA_HEADS):
        kn = _dot(c, wuk_ref[:, hd * QK_NOPE:(hd + 1) * QK_NOPE])
        r = lax.rsqrt((jnp.sum(kn * kn, axis=-1, keepdims=True) + kr_ss) * (1.0 / QK_HEAD) + NORM_EPS)
        k_ref[:, hd * HEAD_PAD:hd * HEAD_PAD + LANES] = (kn * r * gn).astype(BF16)
        rot = rot_a if hd % 2 == 0 else rot_b
        k_ref[:, hd * HEAD_PAD + LANES:(hd + 1) * HEAD_PAD] = (rot * r).astype(BF16)
    v_ref[...] = _dot(c, wuv_ref[...]).astype(BF16)


def _shared_kv(h, g, wd, cg, wuk, wuv, gn, gr, cm, sm):
    t = h.shape[0]
    return pl.pallas_call(
        _kv_kernel,
        grid=(t // ROW_TILE,),
        in_specs=[_rows(D_MODEL), _resident((1, D_MODEL)), _resident((D_MODEL, KV_LORA + LANES)),
                  _resident((1, KV_LORA)), _resident((KV_LORA, MLA_HEADS * QK_NOPE)),
                  _resident((KV_LORA, MLA_HEADS * V_HEAD)), _resident((1, LANES)), _resident((1, LANES)),
                  _rows(LANES), _rows(LANES)],
        out_specs=[_rows(MLA_HEADS * HEAD_PAD), _rows(MLA_HEADS * V_HEAD)],
        out_shape=[jax.ShapeDtypeStruct((t, MLA_HEADS * HEAD_PAD), BF16),
                   jax.ShapeDtypeStruct((t, MLA_HEADS * V_HEAD), BF16)],
        compiler_params=_cparams("parallel"),
        name="shared_kv",
    )(h, g, wd, cg, wuk, wuv, gn, gr, cm, sm)


def _q_proj_kernel(h_ref, g_ref, wd_ref, lg_ref, wu_ref, gn_ref, g2_ref, g3_ref, cm_ref, sm_ref, q_ref):
    hn = (_rms(h_ref[...]) * g_ref[...]).astype(BF16)
    cq = (_rms(_dot(hn, wd_ref[...])) * lg_ref[...]).astype(BF16)
    gn = gn_ref[...]
    gc = g2_ref[...] * cm_ref[...]
    gs = g3_ref[...] * sm_ref[...]
    head_a = _lane_mask(0, QK_ROPE)
    for pair in range(MLA_HEADS // 2):
        qh = _dot(cq, wu_ref[:, pair * 2 * HEAD_PAD:(pair + 1) * 2 * HEAD_PAD])
        na, nb = qh[:, :LANES], qh[:, LANES:2 * LANES]
        xr, xp = qh[:, 2 * LANES:3 * LANES], qh[:, 3 * LANES:]
        sq = xr * xr
        ss_a = jnp.sum(na * na + jnp.where(head_a, sq, 0.0), axis=-1, keepdims=True)
        ss_b = jnp.sum(nb * nb + jnp.where(head_a, 0.0, sq), axis=-1, keepdims=True)
        r_a = lax.rsqrt(ss_a * (1.0 / QK_HEAD) + NORM_EPS) * ATTN_SCALE
        r_b = lax.rsqrt(ss_b * (1.0 / QK_HEAD) + NORM_EPS) * ATTN_SCALE
        rot = ((xr * gc + xp * gs) * jnp.where(head_a, r_a, r_b)).astype(BF16)
        lo = pair * 2 * HEAD_PAD
        q_ref[:, lo:lo + LANES] = (na * r_a * gn).astype(BF16)
        q_ref[:, lo + LANES:lo + HEAD_PAD] = rot
        q_ref[:, lo + HEAD_PAD:lo + HEAD_PAD + LANES] = (nb * r_b * gn).astype(BF16)
        q_ref[:, lo + HEAD_PAD + LANES:lo + 2 * HEAD_PAD] = rot


def _q_proj(h, g, wd, lg, wu, gn, g2, g3, cm, sm):
    t = h.shape[0]
    return pl.pallas_call(
        _q_proj_kernel,
        grid=(t // ROW_TILE,),
        in_specs=[_rows(D_MODEL), _resident((1, D_MODEL)), _resident((D_MODEL, Q_LORA)),
                  _resident((1, Q_LORA)), _resident((Q_LORA, MLA_HEADS * HEAD_PAD)),
                  _resident((1, LANES)), _resident((1, LANES)), _resident((1, LANES)),
                  _rows(LANES), _rows(LANES)],
        out_specs=_rows(MLA_HEADS * HEAD_PAD),
        out_shape=jax.ShapeDtypeStruct((t, MLA_HEADS * HEAD_PAD), BF16),
        compiler_params=_cparams("parallel"),
        name="q_proj",
    )(h, g, wd, lg, wu, gn, g2, g3, cm, sm)


def _pair_up_proj(w):
    w = w.reshape(Q_LORA, MLA_HEADS // 2, 2, QK_HEAD)
    nope = w[..., :QK_NOPE]
    x1 = w[..., QK_NOPE:QK_NOPE + ROPE_HALF]
    x2 = w[..., QK_NOPE + ROPE_HALF:]
    a, b = 0, 1
    slab = jnp.concatenate(
        [nope[:, :, a], nope[:, :, b],
         x1[:, :, a], x2[:, :, a], x1[:, :, b], x2[:, :, b],
         x2[:, :, a], x1[:, :, a], x2[:, :, b], x1[:, :, b]], axis=-1)
    return slab.reshape(Q_LORA, MLA_HEADS * HEAD_PAD)


def _attn_kernel(q_ref, k_ref, v_ref, o_ref, vaug_ref):
    seq = k_ref.shape[0]
    tq = ATTN_TQ
    vaug_ref[:, :V_HEAD] = v_ref[...]
    vaug_ref[:, V_HEAD:] = jnp.ones((seq, HEAD_PAD - V_HEAD), BF16)
    row = lax.broadcasted_iota(jnp.int32, (tq, tq), 0)
    col = lax.broadcasted_iota(jnp.int32, (tq, tq), 1)
    causal = col <= row
    for qi in reversed(range(seq // tq)):
        n0 = qi * tq
        q = q_ref[n0:n0 + tq, :]
        s_diag = jnp.where(causal, _dot_nt(q, k_ref[n0:n0 + tq, :]), -jnp.inf)
        m = jnp.max(s_diag, axis=-1, keepdims=True)
        if qi > 0:
            s_off = _dot_nt(q, k_ref[:n0, :])
            m = jnp.maximum(m, jnp.max(s_off, axis=-1, keepdims=True))
            p = jnp.concatenate([jnp.exp(s_off - m), jnp.exp(s_diag - m)], axis=1)
        else:
            p = jnp.exp(s_diag - m)
        oa = _dot(p.astype(BF16), vaug_ref[:n0 + tq, :])
        o_ref[n0:n0 + tq, :] = (oa[:, :V_HEAD] / oa[:, V_HEAD:]).astype(BF16)


def _attention(q, k, v, batch, seq):
    q3 = q.reshape(batch, seq, MLA_HEADS * HEAD_PAD)
    k3 = k.reshape(batch, seq, MLA_HEADS * HEAD_PAD)
    v3 = v.reshape(batch, seq, MLA_HEADS * V_HEAD)
    qk_spec = pl.BlockSpec((None, seq, HEAD_PAD), lambda b, h: (b, 0, h))
    v_spec = pl.BlockSpec((None, seq, V_HEAD), lambda b, h: (b, 0, h))
    out = pl.pallas_call(
        _attn_kernel,
        grid=(batch, MLA_HEADS),
        in_specs=[qk_spec, qk_spec, v_spec],
        out_specs=v_spec,
        out_shape=jax.ShapeDtypeStruct((batch, seq, MLA_HEADS * V_HEAD), BF16),
        scratch_shapes=[pltpu.VMEM((seq, HEAD_PAD), BF16)],
        compiler_params=_cparams("parallel", "parallel"),
        name="attention",
    )(q3, k3, v3)
    return out.reshape(batch * seq, MLA_HEADS * V_HEAD)


def _pad_lanes(a, width):
    return jnp.pad(a, [(0, 0)] * (a.ndim - 1) + [(0, width - a.shape[-1])])


def kernel(x, positions, attn_norm_g, ffn_norm_g, ffn_w_in, ffn_w_out, ret_w_in, ret_w_out,
           kv_norm_g, w_dkv, ckv_norm_g, w_uk, w_uv, k_norm_g,
           mla_w_dq, q_lora_norm_g, mla_w_uq, q_norm_g, mla_w_o):
    batch, seq, _ = x.shape
    t = batch * seq
    row = lambda a: a.reshape(1, -1)

    cr, sr, cm, sm = _rope_tables(positions)
    log_gamma = jnp.log(1.0 - 2.0 ** (-5.0 - jnp.arange(RET_HEADS, dtype=F32)))

    h = x.reshape(t, D_MODEL)
    for l in range(DEPTH):
        if l < N_RET_LAYERS:
            q, k, v, gs = _ret_proj(log_gamma, h, row(attn_norm_g[l]), ret_w_in[l].astype(BF16), cr, sr)
            y = _retention(log_gamma, q, k, v, gs, batch, seq)
            w_mix = ret_w_out[l]
        else:
            j = l - N_RET_LAYERS
            gx1 = q_norm_g[j][QK_NOPE:QK_NOPE + ROPE_HALF]
            gx2 = q_norm_g[j][QK_NOPE + ROPE_HALF:]
            q = _q_proj(h, row(attn_norm_g[l]), mla_w_dq[j].astype(BF16), row(q_lora_norm_g[j]),
                        _pair_up_proj(mla_w_uq[j]).astype(BF16), row(q_norm_g[j][:QK_NOPE]),
                        row(jnp.concatenate([gx1, gx2, gx1, gx2])), row(jnp.concatenate([gx2, gx1, gx2, gx1])),
                        cm, sm)
            y = _attention(q, shared_k, shared_v, batch, seq)
            w_mix = mla_w_o[j]
        h = _mix_ffn(y, w_mix.astype(BF16), h, row(ffn_norm_g[l]),
                     ffn_w_in[l].astype(BF16), ffn_w_out[l].astype(BF16))
        if l == N_RET_LAYERS - 1:
            shared_k, shared_v = _shared_kv(
                h, row(kv_norm_g), _pad_lanes(w_dkv, KV_LORA + LANES).astype(BF16), row(ckv_norm_g),
                w_uk.astype(BF16), w_uv.astype(BF16),
                row(k_norm_g[:QK_NOPE]), row(_pad_lanes(k_norm_g[QK_NOPE:], LANES)), cm, sm)
    return h.reshape(batch, seq, D_MODEL)
```

```python
import math

import jax
import jax.numpy as jnp
from jax import lax
from jax.experimental import pallas as pl
from jax.experimental.pallas import tpu as pltpu

F32 = jnp.float32
BF16 = jnp.bfloat16

D_MODEL = 1024
DEPTH = 4
N_RET_LAYERS = DEPTH // 2
RET_HEADS = 4
RET_DK = 256
RET_DV = 512
RET_QK = RET_HEADS * RET_DK
RET_V = RET_HEADS * RET_DV
RET_IN = 2 * RET_QK + 2 * RET_V
MLA_HEADS = 16
QK_NOPE = 128
QK_ROPE = 64
QK_HEAD = QK_NOPE + QK_ROPE
ROPE_HALF = QK_ROPE // 2
V_HEAD = 128
Q_LORA = 384
KV_LORA = 256
FFN_HIDDEN = 2816
ROPE_BASE = 10000.0
NORM_EPS = 1e-6
ATTN_SCALE = 1.0 / math.sqrt(QK_HEAD)

LANES = 128
MXU_DIM = 256
HEAD_PAD = 2 * LANES
VMEM_LIMIT_BYTES = 56 * 1024 * 1024

ROW_TILE = 512
FFN_CHUNK = MXU_DIM
RET_CHUNK = 256
RET_ROWS = 1024
ATTN_TQ = 512
PROJ_TILE = 1024


def _cparams(*sem):
    return pltpu.CompilerParams(dimension_semantics=sem, vmem_limit_bytes=VMEM_LIMIT_BYTES)


def _resident(shape):
    zeros = (0,) * len(shape)
    return pl.BlockSpec(shape, lambda *_: zeros, pipeline_mode=pl.Buffered(1))


def _rows(width, tile=ROW_TILE):
    return pl.BlockSpec((tile, width), lambda i: (i, 0))


def _rms(x):
    return x * lax.rsqrt(jnp.mean(x * x, axis=-1, keepdims=True) + NORM_EPS)


def _dot(a, b):
    return jnp.dot(a, b, preferred_element_type=F32)


def _dot_nt(a, b):
    return lax.dot_general(a, b, (((1,), (1,)), ((), ())), preferred_element_type=F32)


def _dot_tn(a, b):
    return lax.dot_general(a, b, (((0,), (0,)), ((), ())), preferred_element_type=F32)


def _select_lanes(x, sel):
    hi = x.astype(BF16)
    rest = x - hi.astype(F32)
    mid = rest.astype(BF16)
    lo = (rest - mid.astype(F32)).astype(BF16)
    return (_dot(hi, sel) + _dot(mid, sel)) + _dot(lo, sel)


def _rope_table_kernel(pos_ref, fr_ref, sel_ref, sg_ref, cr_ref, sr_ref, cm_ref, sm_ref):
    p = pos_ref[...].astype(F32)
    ar = p * fr_ref[...]
    c = jnp.cos(ar)
    s = jnp.sin(ar)
    cr_ref[...] = c
    sr_ref[...] = s
    sel = sel_ref[...]
    cm_ref[...] = _select_lanes(c, sel)
    sm_ref[...] = _select_lanes(s, sel) * sg_ref[...]


def _rope_tables(positions):
    t = positions.size
    pos = positions.reshape(t, 1)
    fr = ROPE_BASE ** (-jnp.arange(0, RET_DK, 2, dtype=F32) / RET_DK)
    stride = RET_DK // QK_ROPE
    src = stride * (jnp.arange(LANES) % ROPE_HALF)
    sel = (jnp.arange(LANES)[:, None] == src[None, :]).astype(BF16)
    sg = jnp.tile(jnp.concatenate([-jnp.ones((ROPE_HALF,), F32), jnp.ones((ROPE_HALF,), F32)]),
                  LANES // QK_ROPE)
    row = lambda a: a.reshape(1, LANES)
    assert RET_DK // 2 == LANES
    tab = jax.ShapeDtypeStruct((t, LANES), F32)
    return pl.pallas_call(
        _rope_table_kernel,
        grid=(t // ROW_TILE,),
        in_specs=[_rows(1), _resident((1, LANES)), _resident((LANES, LANES)), _resident((1, LANES))],
        out_specs=[_rows(LANES)] * 4,
        out_shape=[tab] * 4,
        compiler_params=_cparams("parallel"),
        name="rope_tables",
    )(pos, row(fr), sel, row(sg))


def _lane_mask(lo, hi):
    lane = lax.broadcasted_iota(jnp.int32, (1, LANES), 1)
    return jnp.logical_and(lane >= lo, lane < hi)


def _ret_proj_kernel(lg_ref, h_ref, g_ref, w_ref, c_ref, s_ref, q_ref, k_ref, v_ref, gs_ref):
    hn = (_rms(h_ref[...]) * g_ref[...]).astype(BF16)
    c = c_ref[...]
    s = s_ref[...]
    half = RET_DK // 2
    in_chunk = jnp.bitwise_and(lax.broadcasted_iota(jnp.int32, (PROJ_TILE, 1), 0), RET_CHUNK - 1)
    back = (RET_CHUNK - 1 - in_chunk).astype(F32)
    for dst, base in ((q_ref, 0), (k_ref, RET_QK)):
        for hd in range(RET_HEADS):
            lo = hd * RET_DK
            p = _dot(hn, w_ref[:, base + lo:base + lo + RET_DK])
            x1, x2 = p[:, :half], p[:, half:]
            scale = 1.0 if dst is q_ref else jnp.exp(back * lg_ref[hd]) * RET_DK ** -0.5
            dst[:, lo:lo + half] = ((x1 * c - x2 * s) * scale).astype(BF16)
            dst[:, lo + half:lo + RET_DK] = ((x2 * c + x1 * s) * scale).astype(BF16)
    for hd in range(RET_HEADS):
        lo = hd * RET_DV
        v_ref[:, lo:lo + RET_DV] = _dot(hn, w_ref[:, 2 * RET_QK + lo:2 * RET_QK + lo + RET_DV]).astype(BF16)
    for hd in range(RET_HEADS):
        lo = hd * RET_DV
        gate = _dot(hn, w_ref[:, 2 * RET_QK + RET_V + lo:2 * RET_QK + RET_V + lo + RET_DV])
        gs_ref[:, lo:lo + RET_DV] = (gate * jax.nn.sigmoid(gate)).astype(BF16)


def _ret_proj(log_gamma, h, g, w, cr, sr):
    t = h.shape[0]
    assert PROJ_TILE % RET_CHUNK == 0
    rows = lambda width: _rows(width, PROJ_TILE)
    return pl.pallas_call(
        _ret_proj_kernel,
        grid=(t // PROJ_TILE,),
        in_specs=[pl.BlockSpec(memory_space=pltpu.SMEM),
                  rows(D_MODEL), _resident((1, D_MODEL)), _resident((D_MODEL, RET_IN)),
                  rows(LANES), rows(LANES)],
        out_specs=[rows(RET_QK), rows(RET_QK), rows(RET_V), rows(RET_V)],
        out_shape=[jax.ShapeDtypeStruct((t, RET_QK), BF16), jax.ShapeDtypeStruct((t, RET_QK), BF16),
                   jax.ShapeDtypeStruct((t, RET_V), BF16), jax.ShapeDtypeStruct((t, RET_V), BF16)],
        compiler_params=_cparams("parallel"),
        name="ret_proj",
    )(log_gamma, h, g, w, cr, sr)


def _retention_kernel(lg_ref, q_ref, k_ref, v_ref, gs_ref, o_ref, state_ref):
    c = RET_CHUNK

    @pl.when(pl.program_id(1) == 0)
    def _():
        state_ref[...] = jnp.zeros_like(state_ref)

    ii = lax.broadcasted_iota(jnp.int32, (c, c), 0)
    jj = lax.broadcasted_iota(jnp.int32, (c, c), 1)
    lower = jj <= ii
    back = (c - 1 - lax.broadcasted_iota(jnp.int32, (c, 1), 0)).astype(F32)

    consts = []
    for hd in range(RET_HEADS):
        lg = lg_ref[hd]
        inv_rho = jnp.exp(back * lg)
        consts.append((NORM_EPS * inv_rho * inv_rho,
                       jnp.exp(jnp.full((1, RET_DV), float(c), F32) * lg)))

    for ci in range(RET_ROWS // c):
        rows = slice(ci * c, (ci + 1) * c)
        for hd in range(RET_HEADS):
            eps_rows, chunk_decay = consts[hd]
            qk_cols = slice(hd * RET_DK, (hd + 1) * RET_DK)
            v_cols = slice(hd * RET_DV, (hd + 1) * RET_DV)
            qc = q_ref[rows, qk_cols]
            kc = k_ref[rows, qk_cols]
            vc = v_ref[rows, v_cols]
            st = state_ref[hd] * chunk_decay
            p = jnp.where(lower, _dot_nt(qc, kc), 0.0).astype(BF16)
            o = _dot(qc, st.astype(BF16)) + _dot(p, vc)
            state_ref[hd] = st + _dot_tn(kc, vc)
            y = o * lax.rsqrt(jnp.mean(o * o, axis=-1, keepdims=True) + eps_rows)
            o_ref[rows, v_cols] = y.astype(BF16) * gs_ref[rows, v_cols]


def _retention(log_gamma, q, k, v, gs, batch, seq):
    q3 = q.reshape(batch, seq, RET_QK)
    k3 = k.reshape(batch, seq, RET_QK)
    v3 = v.reshape(batch, seq, RET_V)
    g3 = gs.reshape(batch, seq, RET_V)
    qk_spec = pl.BlockSpec((None, RET_ROWS, RET_QK), lambda b, s: (b, s, 0))
    v_spec = pl.BlockSpec((None, RET_ROWS, RET_V), lambda b, s: (b, s, 0))
    out = pl.pallas_call(
        _retention_kernel,
        grid=(batch, seq // RET_ROWS),
        in_specs=[pl.BlockSpec(memory_space=pltpu.SMEM), qk_spec, qk_spec, v_spec, v_spec],
        out_specs=v_spec,
        out_shape=jax.ShapeDtypeStruct((batch, seq, RET_V), BF16),
        scratch_shapes=[pltpu.VMEM((RET_HEADS, RET_DK, RET_DV), F32)],
        compiler_params=_cparams("parallel", "arbitrary"),
        name="retention",
    )(log_gamma, q3, k3, v3, g3)
    return out.reshape(batch * seq, RET_V)


def _mix_ffn_kernel(y_ref, wm_ref, h_ref, g_ref, wi_ref, wo_ref, o_ref, act_ref):
    x = h_ref[...] + _dot(y_ref[...], wm_ref[...])
    hn = (_rms(x) * g_ref[...]).astype(BF16)
    for f in range(FFN_HIDDEN // FFN_CHUNK):
        lo = f * FFN_CHUNK
        a = _dot(hn, wi_ref[:, lo:lo + FFN_CHUNK])
        b = _dot(hn, wi_ref[:, FFN_HIDDEN + lo:FFN_HIDDEN + lo + FFN_CHUNK])
        act_ref[:, lo:lo + FFN_CHUNK] = (a * jax.nn.sigmoid(a) * b).astype(BF16)
    o_ref[...] = x + _dot(act_ref[...], wo_ref[...])


def _mix_ffn(y, w_mix, h, g, w_in, w_out):
    t, kdim = y.shape
    return pl.pallas_call(
        _mix_ffn_kernel,
        grid=(t // ROW_TILE,),
        in_specs=[_rows(kdim), _resident((kdim, D_MODEL)), _rows(D_MODEL), _resident((1, D_MODEL)),
                  _resident((D_MODEL, 2 * FFN_HIDDEN)), _resident((FFN_HIDDEN, D_MODEL))],
        out_specs=_rows(D_MODEL),
        out_shape=jax.ShapeDtypeStruct((t, D_MODEL), F32),
        scratch_shapes=[pltpu.VMEM((ROW_TILE, FFN_HIDDEN), BF16)],
        compiler_params=_cparams("parallel"),
        name="mix_ffn",
    )(y, w_mix, h, g, w_in, w_out)


def _kv_kernel(h_ref, g_ref, wd_ref, cg_ref, wuk_ref, wuv_ref, gn_ref, gr_ref,
               cm_ref, sm_ref, k_ref, v_ref):
    hn = (_rms(h_ref[...]) * g_ref[...]).astype(BF16)
    ckr = _dot(hn, wd_ref[...])
    c = (_rms(ckr[:, :KV_LORA]) * cg_ref[...]).astype(BF16)
    kr = ckr[:, KV_LORA:]
    kr_ss = jnp.sum(kr * kr, axis=-1, keepdims=True)
    krg = kr * gr_ref[...]
    sm = sm_ref[...]
    sa = jnp.where(_lane_mask(0, ROPE_HALF), sm, 0.0)
    sb = jnp.where(_lane_mask(ROPE_HALF, QK_ROPE), sm, 0.0)
    rot_a = krg * cm_ref[...] + pltpu.roll(krg, LANES - ROPE_HALF, 1) * sa + pltpu.roll(krg, ROPE_HALF, 1) * sb
    rot_b = pltpu.roll(rot_a, QK_ROPE, 1)
    gn = gn_ref[...]
    for hd in range(MLA_HEADS):
        kn = _dot(c, wuk_ref[:, hd * QK_NOPE:(hd + 1) * QK_NOPE])
        r = lax.rsqrt((jnp.sum(kn * kn, axis=-1, keepdims=True) + kr_ss) * (1.0 / QK_HEAD) + NORM_EPS)
        k_ref[:, hd * HEAD_PAD:hd * HEAD_PAD + LANES] = (kn * r * gn).astype(BF16)
        rot = rot_a if hd % 2 == 0 else rot_b
        k_ref[:, hd * HEAD_PAD + LANES:(hd + 1) * HEAD_PAD] = (rot * r).astype(BF16)
    v_ref[...] = _dot(c, wuv_ref[...]).astype(BF16)


def _shared_kv(h, g, wd, cg, wuk, wuv, gn, gr, cm, sm):
    t = h.shape[0]
    return pl.pallas_call(
        _kv_kernel,
        grid=(t // ROW_TILE,),
        in_specs=[_rows(D_MODEL), _resident((1, D_MODEL)), _resident((D_MODEL, KV_LORA + LANES)),
                  _resident((1, KV_LORA)), _resident((KV_LORA, MLA_HEADS * QK_NOPE)),
                  _resident((KV_LORA, MLA_HEADS * V_HEAD)), _resident((1, LANES)), _resident((1, LANES)),
                  _rows(LANES), _rows(LANES)],
        out_specs=[_rows(MLA_HEADS * HEAD_PAD), _rows(MLA_HEADS * V_HEAD)],
        out_shape=[jax.ShapeDtypeStruct((t, MLA_HEADS * HEAD_PAD), BF16),
                   jax.ShapeDtypeStruct((t, MLA_HEADS * V_HEAD), BF16)],
        compiler_params=_cparams("parallel"),
        name="shared_kv",
    )(h, g, wd, cg, wuk, wuv, gn, gr, cm, sm)


def _q_proj_kernel(h_ref, g_ref, wd_ref, lg_ref, wu_ref, gn_ref, g2_ref, g3_ref, cm_ref, sm_ref, q_ref):
    hn = (_rms(h_ref[...]) * g_ref[...]).astype(BF16)
    cq = (_rms(_dot(hn, wd_ref[...])) * lg_ref[...]).astype(BF16)
    gn = gn_ref[...]
    gc = g2_ref[...] * cm_ref[...]
    gs = g3_ref[...] * sm_ref[...]
    head_a = _lane_mask(0, QK_ROPE)
    for pair in range(MLA_HEADS // 2):
        qh = _dot(cq, wu_ref[:, pair * 2 * HEAD_PAD:(pair + 1) * 2 * HEAD_PAD])
        na, nb = qh[:, :LANES], qh[:, LANES:2 * LANES]
        xr, xp = qh[:, 2 * LANES:3 * LANES], qh[:, 3 * LANES:]
        sq = xr * xr
        ss_a = jnp.sum(na * na + jnp.where(head_a, sq, 0.0), axis=-1, keepdims=True)
        ss_b = jnp.sum(nb * nb + jnp.where(head_a, 0.0, sq), axis=-1, keepdims=True)
        r_a = lax.rsqrt(ss_a * (1.0 / QK_HEAD) + NORM_EPS) * ATTN_SCALE
        r_b = lax.rsqrt(ss_b * (1.0 / QK_HEAD) + NORM_EPS) * ATTN_SCALE
        rot = ((xr * gc + xp * gs) * jnp.where(head_a, r_a, r_b)).astype(BF16)
        lo = pair * 2 * HEAD_PAD
        q_ref[:, lo:lo + LANES] = (na * r_a * gn).astype(BF16)
        q_ref[:, lo + LANES:lo + HEAD_PAD] = rot
        q_ref[:, lo + HEAD_PAD:lo + HEAD_PAD + LANES] = (nb * r_b * gn).astype(BF16)
        q_ref[:, lo + HEAD_PAD + LANES:lo + 2 * HEAD_PAD] = rot


def _q_proj(h, g, wd, lg, wu, gn, g2, g3, cm, sm):
    t = h.shape[0]
    return pl.pallas_call(
        _q_proj_kernel,
        grid=(t // ROW_TILE,),
        in_specs=[_rows(D_MODEL), _resident((1, D_MODEL)), _resident((D_MODEL, Q_LORA)),
                  _resident((1, Q_LORA)), _resident((Q_LORA, MLA_HEADS * HEAD_PAD)),
                  _resident((1, LANES)), _resident((1, LANES)), _resident((1, LANES)),
                  _rows(LANES), _rows(LANES)],
        out_specs=_rows(MLA_HEADS * HEAD_PAD),
        out_shape=jax.ShapeDtypeStruct((t, MLA_HEADS * HEAD_PAD), BF16),
        compiler_params=_cparams("parallel"),
        name="q_proj",
    )(h, g, wd, lg, wu, gn, g2, g3, cm, sm)


def _pair_up_proj(w):
    w = w.reshape(Q_LORA, MLA_HEADS // 2, 2, QK_HEAD)
    nope = w[..., :QK_NOPE]
    x1 = w[..., QK_NOPE:QK_NOPE + ROPE_HALF]
    x2 = w[..., QK_NOPE + ROPE_HALF:]
    a, b = 0, 1
    slab = jnp.concatenate(
        [nope[:, :, a], nope[:, :, b],
         x1[:, :, a], x2[:, :, a], x1[:, :, b], x2[:, :, b],
         x2[:, :, a], x1[:, :, a], x2[:, :, b], x1[:, :, b]], axis=-1)
    return slab.reshape(Q_LORA, MLA_HEADS * HEAD_PAD)


def _attn_kernel(q_ref, k_ref, v_ref, o_ref, vaug_ref):
    seq = k_ref.shape[0]
    tq = ATTN_TQ
    vaug_ref[:, :V_HEAD] = v_ref[...]
    vaug_ref[:, V_HEAD:] = jnp.ones((seq, HEAD_PAD - V_HEAD), BF16)
    row = lax.broadcasted_iota(jnp.int32, (tq, tq), 0)
    col = lax.broadcasted_iota(jnp.int32, (tq, tq), 1)
    causal = col <= row
    for qi in reversed(range(seq // tq)):
        n0 = qi * tq
        q = q_ref[n0:n0 + tq, :]
        s_diag = jnp.where(causal, _dot_nt(q, k_ref[n0:n0 + tq, :]), -jnp.inf)
        m = jnp.max(s_diag, axis=-1, keepdims=True)
        if qi > 0:
            s_off = _dot_nt(q, k_ref[:n0, :])
            m = jnp.maximum(m, jnp.max(s_off, axis=-1, keepdims=True))
            p = jnp.concatenate([jnp.exp(s_off - m), jnp.exp(s_diag - m)], axis=1)
        else:
            p = jnp.exp(s_diag - m)
        oa = _dot(p.astype(BF16), vaug_ref[:n0 + tq, :])
        o_ref[n0:n0 + tq, :] = (oa[:, :V_HEAD] / oa[:, V_HEAD:]).astype(BF16)


def _attention(q, k, v, batch, seq):
    q3 = q.reshape(batch, seq, MLA_HEADS * HEAD_PAD)
    k3 = k.reshape(batch, seq, MLA_HEADS * HEAD_PAD)
    v3 = v.reshape(batch, seq, MLA_HEADS * V_HEAD)
    qk_spec = pl.BlockSpec((None, seq, HEAD_PAD), lambda b, h: (b, 0, h))
    v_spec = pl.BlockSpec((None, seq, V_HEAD), lambda b, h: (b, 0, h))
    out = pl.pallas_call(
        _attn_kernel,
        grid=(batch, MLA_HEADS),
        in_specs=[qk_spec, qk_spec, v_spec],
        out_specs=v_spec,
        out_shape=jax.ShapeDtypeStruct((batch, seq, MLA_HEADS * V_HEAD), BF16),
        scratch_shapes=[pltpu.VMEM((seq, HEAD_PAD), BF16)],
        compiler_params=_cparams("parallel", "parallel"),
        name="attention",
    )(q3, k3, v3)
    return out.reshape(batch * seq, MLA_HEADS * V_HEAD)


def _pad_lanes(a, width):
    return jnp.pad(a, [(0, 0)] * (a.ndim - 1) + [(0, width - a.shape[-1])])


def kernel(x, positions, attn_norm_g, ffn_norm_g, ffn_w_in, ffn_w_out, ret_w_in, ret_w_out,
           kv_norm_g, w_dkv, ckv_norm_g, w_uk, w_uv, k_norm_g,
           mla_w_dq, q_lora_norm_g, mla_w_uq, q_norm_g, mla_w_o):
    batch, seq, _ = x.shape
    t = batch * seq
    row = lambda a: a.reshape(1, -1)

    cr, sr, cm, sm = _rope_tables(positions)
    log_gamma = jnp.log(1.0 - 2.0 ** (-5.0 - jnp.arange(RET_HEADS, dtype=F32)))

    h = x.reshape(t, D_MODEL)
    for l in range(DEPTH):
        if l < N_RET_LAYERS:
            q, k, v, gs = _ret_proj(log_gamma, h, row(attn_norm_g[l]), ret_w_in[l].astype(BF16), cr, sr)
            y = _retention(log_gamma, q, k, v, gs, batch, seq)
            w_mix = ret_w_out[l]
        else:
            j = l - N_RET_LAYERS
            gx1 = q_norm_g[j][QK_NOPE:QK_NOPE + ROPE_HALF]
            gx2 = q_norm_g[j][QK_NOPE + ROPE_HALF:]
            q = _q_proj(h, row(attn_norm_g[l]), mla_w_dq[j].astype(BF16), row(q_lora_norm_g[j]),
                        _pair_up_proj(mla_w_uq[j]).astype(BF16), row(q_norm_g[j][:QK_NOPE]),
                        row(jnp.concatenate([gx1, gx2, gx1, gx2])), row(jnp.concatenate([gx2, gx1, gx2, gx1])),
                        cm, sm)
            y = _attention(q, shared_k, shared_v, batch, seq)
            w_mix = mla_w_o[j]
        h = _mix_ffn(y, w_mix.astype(BF16), h, row(ffn_norm_g[l]),
                     ffn_w_in[l].astype(BF16), ffn_w_out[l].astype(BF16))
        if l == N_RET_LAYERS - 1:
            shared_k, shared_v = _shared_kv(
                h, row(kv_norm_g), _pad_lanes(w_dkv, KV_LORA + LANES).astype(BF16), row(ckv_norm_g),
                w_uk.astype(BF16), w_uv.astype(BF16),
                row(k_norm_g[:QK_NOPE]), row(_pad_lanes(k_norm_g[QK_NOPE:], LANES)), cm, sm)
    return h.reshape(batch, seq, D_MODEL)
```

```python
import math

import jax
import jax.numpy as jnp
from jax import lax
from jax.experimental import pallas as pl
from jax.experimental.pallas import tpu as pltpu

F32 = jnp.float32
BF16 = jnp.bfloat16

D_MODEL = 1024
DEPTH = 4
N_RET_LAYERS = DEPTH // 2
RET_HEADS = 4
RET_DK = 256
RET_DV = 512
RET_QK = RET_HEADS * RET_DK
RET_V = RET_HEADS * RET_DV
RET_IN = 2 * RET_QK + 2 * RET_V
MLA_HEADS = 16
QK_NOPE = 128
QK_ROPE = 64
QK_HEAD = QK_NOPE + QK_ROPE
ROPE_HALF = QK_ROPE // 2
V_HEAD = 128
Q_LORA = 384
KV_LORA = 256
FFN_HIDDEN = 2816
ROPE_BASE = 10000.0
NORM_EPS = 1e-6
ATTN_SCALE = math.log2(math.e) / math.sqrt(QK_HEAD)

LANES = 128
MXU_DIM = 256
HEAD_PAD = 2 * LANES
VMEM_LIMIT_BYTES = 56 * 1024 * 1024

ROW_TILE = 512
FFN_CHUNK = MXU_DIM
RET_CHUNK = 256
RET_ROWS = 1024
ATTN_TQ = 512
PROJ_TILE = 1024


def _cparams(*sem):
    return pltpu.CompilerParams(dimension_semantics=sem, vmem_limit_bytes=VMEM_LIMIT_BYTES)


def _resident(shape):
    zeros = (0,) * len(shape)
    return pl.BlockSpec(shape, lambda *_: zeros, pipeline_mode=pl.Buffered(1))


def _rows(width, tile=ROW_TILE):
    return pl.BlockSpec((tile, width), lambda i: (i, 0))


def _rms(x):
    return x * lax.rsqrt(jnp.mean(x * x, axis=-1, keepdims=True) + NORM_EPS)


def _dot(a, b):
    return jnp.dot(a, b, preferred_element_type=F32)


def _dot_nt(a, b):
    return lax.dot_general(a, b, (((1,), (1,)), ((), ())), preferred_element_type=F32)


def _dot_tn(a, b):
    return lax.dot_general(a, b, (((0,), (0,)), ((), ())), preferred_element_type=F32)


def _select_lanes(x, sel):
    hi = x.astype(BF16)
    rest = x - hi.astype(F32)
    mid = rest.astype(BF16)
    lo = (rest - mid.astype(F32)).astype(BF16)
    return (_dot(hi, sel) + _dot(mid, sel)) + _dot(lo, sel)


def _rope_table_kernel(pos_ref, fr_ref, sel_ref, sg_ref, cr_ref, sr_ref, cm_ref, sm_ref):
    p = pos_ref[...].astype(F32)
    ar = p * fr_ref[...]
    c = jnp.cos(ar)
    s = jnp.sin(ar)
    cr_ref[...] = c
    sr_ref[...] = s
    sel = sel_ref[...]
    cm_ref[...] = _select_lanes(c, sel)
    sm_ref[...] = _select_lanes(s, sel) * sg_ref[...]


def _rope_tables(positions):
    t = positions.size
    pos = positions.reshape(t, 1)
    fr = ROPE_BASE ** (-jnp.arange(0, RET_DK, 2, dtype=F32) / RET_DK)
    stride = RET_DK // QK_ROPE
    src = stride * (jnp.arange(LANES) % ROPE_HALF)
    sel = (jnp.arange(LANES)[:, None] == src[None, :]).astype(BF16)
    sg = jnp.tile(jnp.concatenate([-jnp.ones((ROPE_HALF,), F32), jnp.ones((ROPE_HALF,), F32)]),
                  LANES // QK_ROPE)
    row = lambda a: a.reshape(1, LANES)
    assert RET_DK // 2 == LANES
    tab = jax.ShapeDtypeStruct((t, LANES), F32)
    return pl.pallas_call(
        _rope_table_kernel,
        grid=(t // ROW_TILE,),
        in_specs=[_rows(1), _resident((1, LANES)), _resident((LANES, LANES)), _resident((1, LANES))],
        out_specs=[_rows(LANES)] * 4,
        out_shape=[tab] * 4,
        compiler_params=_cparams("parallel"),
        name="rope_tables",
    )(pos, row(fr), sel, row(sg))


def _lane_mask(lo, hi):
    lane = lax.broadcasted_iota(jnp.int32, (1, LANES), 1)
    return jnp.logical_and(lane >= lo, lane < hi)


def _ret_proj_kernel(lg_ref, h_ref, g_ref, w_ref, c_ref, s_ref, q_ref, k_ref, v_ref, gs_ref):
    hn = (_rms(h_ref[...]) * g_ref[...]).astype(BF16)
    c = c_ref[...]
    s = s_ref[...]
    half = RET_DK // 2
    in_chunk = jnp.bitwise_and(lax.broadcasted_iota(jnp.int32, (PROJ_TILE, 1), 0), RET_CHUNK - 1)
    back = (RET_CHUNK - 1 - in_chunk).astype(F32)
    for dst, base in ((q_ref, 0), (k_ref, RET_QK)):
        for hd in range(RET_HEADS):
            lo = hd * RET_DK
            p = _dot(hn, w_ref[:, base + lo:base + lo + RET_DK])
            x1, x2 = p[:, :half], p[:, half:]
            scale = 1.0 if dst is q_ref else jnp.exp(back * lg_ref[hd]) * RET_DK ** -0.5
            dst[:, lo:lo + half] = ((x1 * c - x2 * s) * scale).astype(BF16)
            dst[:, lo + half:lo + RET_DK] = ((x2 * c + x1 * s) * scale).astype(BF16)
    for hd in range(RET_HEADS):
        lo = hd * RET_DV
        v_ref[:, lo:lo + RET_DV] = _dot(hn, w_ref[:, 2 * RET_QK + lo:2 * RET_QK + lo + RET_DV]).astype(BF16)
    for hd in range(RET_HEADS):
        lo = hd * RET_DV
        gate = _dot(hn, w_ref[:, 2 * RET_QK + RET_V + lo:2 * RET_QK + RET_V + lo + RET_DV])
        gs_ref[:, lo:lo + RET_DV] = (gate * jax.nn.sigmoid(gate)).astype(BF16)


def _ret_proj(log_gamma, h, g, w, cr, sr):
    t = h.shape[0]
    assert PROJ_TILE % RET_CHUNK == 0
    rows = lambda width: _rows(width, PROJ_TILE)
    return pl.pallas_call(
        _ret_proj_kernel,
        grid=(t // PROJ_TILE,),
        in_specs=[pl.BlockSpec(memory_space=pltpu.SMEM),
                  rows(D_MODEL), _resident((1, D_MODEL)), _resident((D_MODEL, RET_IN)),
                  rows(LANES), rows(LANES)],
        out_specs=[rows(RET_QK), rows(RET_QK), rows(RET_V), rows(RET_V)],
        out_shape=[jax.ShapeDtypeStruct((t, RET_QK), BF16), jax.ShapeDtypeStruct((t, RET_QK), BF16),
                   jax.ShapeDtypeStruct((t, RET_V), BF16), jax.ShapeDtypeStruct((t, RET_V), BF16)],
        compiler_params=_cparams("parallel"),
        name="ret_proj",
    )(log_gamma, h, g, w, cr, sr)


def _retention_kernel(lg_ref, q_ref, k_ref, v_ref, gs_ref, o_ref, state_ref):
    c = RET_CHUNK

    @pl.when(pl.program_id(1) == 0)
    def _():
        state_ref[...] = jnp.zeros_like(state_ref)

    ii = lax.broadcasted_iota(jnp.int32, (c, c), 0)
    jj = lax.broadcasted_iota(jnp.int32, (c, c), 1)
    lower = jj <= ii
    back = (c - 1 - lax.broadcasted_iota(jnp.int32, (c, 1), 0)).astype(F32)

    consts = []
    for hd in range(RET_HEADS):
        lg = lg_ref[hd]
        inv_rho = jnp.exp(back * lg)
        consts.append((NORM_EPS * inv_rho * inv_rho,
                       jnp.exp(jnp.full((1, RET_DV), float(c), F32) * lg)))

    for ci in range(RET_ROWS // c):
        rows = slice(ci * c, (ci + 1) * c)
        for hd in range(RET_HEADS):
            eps_rows, chunk_decay = consts[hd]
            qk_cols = slice(hd * RET_DK, (hd + 1) * RET_DK)
            v_cols = slice(hd * RET_DV, (hd + 1) * RET_DV)
            qc = q_ref[rows, qk_cols]
            kc = k_ref[rows, qk_cols]
            vc = v_ref[rows, v_cols]
            st = state_ref[hd] * chunk_decay
            p = jnp.where(lower, _dot_nt(qc, kc), 0.0).astype(BF16)
            o = _dot(qc, st.astype(BF16)) + _dot(p, vc)
            state_ref[hd] = st + _dot_tn(kc, vc)
            y = o * lax.rsqrt(jnp.mean(o * o, axis=-1, keepdims=True) + eps_rows)
            o_ref[rows, v_cols] = y.astype(BF16) * gs_ref[rows, v_cols]


def _retention(log_gamma, q, k, v, gs, batch, seq):
    q3 = q.reshape(batch, seq, RET_QK)
    k3 = k.reshape(batch, seq, RET_QK)
    v3 = v.reshape(batch, seq, RET_V)
    g3 = gs.reshape(batch, seq, RET_V)
    qk_spec = pl.BlockSpec((None, RET_ROWS, RET_QK), lambda b, s: (b, s, 0))
    v_spec = pl.BlockSpec((None, RET_ROWS, RET_V), lambda b, s: (b, s, 0))
    out = pl.pallas_call(
        _retention_kernel,
        grid=(batch, seq // RET_ROWS),
        in_specs=[pl.BlockSpec(memory_space=pltpu.SMEM), qk_spec, qk_spec, v_spec, v_spec],
        out_specs=v_spec,
        out_shape=jax.ShapeDtypeStruct((batch, seq, RET_V), BF16),
        scratch_shapes=[pltpu.VMEM((RET_HEADS, RET_DK, RET_DV), F32)],
        compiler_params=_cparams("parallel", "arbitrary"),
        name="retention",
    )(log_gamma, q3, k3, v3, g3)
    return out.reshape(batch * seq, RET_V)


def _mix_ffn_kernel(y_ref, wm_ref, h_ref, g_ref, wi_ref, wo_ref, o_ref, act_ref):
    x = h_ref[...] + _dot(y_ref[...], wm_ref[...])
    hn = (_rms(x) * g_ref[...]).astype(BF16)
    for f in range(FFN_HIDDEN // FFN_CHUNK):
        lo = f * FFN_CHUNK
        a = _dot(hn, wi_ref[:, lo:lo + FFN_CHUNK])
        b = _dot(hn, wi_ref[:, FFN_HIDDEN + lo:FFN_HIDDEN + lo + FFN_CHUNK])
        act_ref[:, lo:lo + FFN_CHUNK] = (a * jax.nn.sigmoid(a) * b).astype(BF16)
    o_ref[...] = x + _dot(act_ref[...], wo_ref[...])


def _mix_ffn(y, w_mix, h, g, w_in, w_out):
    t, kdim = y.shape
    return pl.pallas_call(
        _mix_ffn_kernel,
        grid=(t // ROW_TILE,),
        in_specs=[_rows(kdim), _resident((kdim, D_MODEL)), _rows(D_MODEL), _resident((1, D_MODEL)),
                  _resident((D_MODEL, 2 * FFN_HIDDEN)), _resident((FFN_HIDDEN, D_MODEL))],
        out_specs=_rows(D_MODEL),
        out_shape=jax.ShapeDtypeStruct((t, D_MODEL), F32),
        scratch_shapes=[pltpu.VMEM((ROW_TILE, FFN_HIDDEN), BF16)],
        compiler_params=_cparams("parallel"),
        name="mix_ffn",
    )(y, w_mix, h, g, w_in, w_out)


def _kv_kernel(h_ref, g_ref, wd_ref, cg_ref, wuk_ref, wuv_ref, gn_ref, gr_ref,
               cm_ref, sm_ref, k_ref, v_ref):
    hn = (_rms(h_ref[...]) * g_ref[...]).astype(BF16)
    ckr = _dot(hn, wd_ref[...])
    c = (_rms(ckr[:, :KV_LORA]) * cg_ref[...]).astype(BF16)
    kr = ckr[:, KV_LORA:]
    kr_ss = jnp.sum(kr * kr, axis=-1, keepdims=True)
    krg = kr * gr_ref[...]
    sm = sm_ref[...]
    sa = jnp.where(_lane_mask(0, ROPE_HALF), sm, 0.0)
    sb = jnp.where(_lane_mask(ROPE_HALF, QK_ROPE), sm, 0.0)
    rot_a = krg * cm_ref[...] + pltpu.roll(krg, LANES - ROPE_HALF, 1) * sa + pltpu.roll(krg, ROPE_HALF, 1) * sb
    rot_b = pltpu.roll(rot_a, QK_ROPE, 1)
    gn = gn_ref[...]
    for hd in range(MLA_HEADS):
        kn = _dot(c, wuk_ref[:, hd * QK_NOPE:(hd + 1) * QK_NOPE])
        r = lax.rsqrt((jnp.sum(kn * kn, axis=-1, keepdims=True) + kr_ss) * (1.0 / QK_HEAD) + NORM_EPS)
        k_ref[:, hd * HEAD_PAD:hd * HEAD_PAD + LANES] = (kn * r * gn).astype(BF16)
        rot = rot_a if hd % 2 == 0 else rot_b
        k_ref[:, hd * HEAD_PAD + LANES:(hd + 1) * HEAD_PAD] = (rot * r).astype(BF16)
    v_ref[...] = _dot(c, wuv_ref[...]).astype(BF16)


def _shared_kv(h, g, wd, cg, wuk, wuv, gn, gr, cm, sm):
    t = h.shape[0]
    rows = lambda width: _rows(width, PROJ_TILE)
    return pl.pallas_call(
        _kv_kernel,
        grid=(t // PROJ_TILE,),
        in_specs=[rows(D_MODEL), _resident((1, D_MODEL)), _resident((D_MODEL, KV_LORA + LANES)),
                  _resident((1, KV_LORA)), _resident((KV_LORA, MLA_HEADS * QK_NOPE)),
                  _resident((KV_LORA, MLA_HEADS * V_HEAD)), _resident((1, LANES)), _resident((1, LANES)),
                  rows(LANES), rows(LANES)],
        out_specs=[rows(MLA_HEADS * HEAD_PAD), rows(MLA_HEADS * V_HEAD)],
        out_shape=[jax.ShapeDtypeStruct((t, MLA_HEADS * HEAD_PAD), BF16),
                   jax.ShapeDtypeStruct((t, MLA_HEADS * V_HEAD), BF16)],
        compiler_params=_cparams("parallel"),
        name="shared_kv",
    )(h, g, wd, cg, wuk, wuv, gn, gr, cm, sm)


def _q_proj_kernel(h_ref, g_ref, wd_ref, lg_ref, wu_ref, gn_ref, g2_ref, g3_ref, cm_ref, sm_ref, q_ref):
    hn = (_rms(h_ref[...]) * g_ref[...]).astype(BF16)
    cq = (_rms(_dot(hn, wd_ref[...])) * lg_ref[...]).astype(BF16)
    gn = gn_ref[...]
    gc = g2_ref[...] * cm_ref[...]
    gs = g3_ref[...] * sm_ref[...]
    head_a = _lane_mask(0, QK_ROPE)
    for pair in range(MLA_HEADS // 2):
        qh = _dot(cq, wu_ref[:, pair * 2 * HEAD_PAD:(pair + 1) * 2 * HEAD_PAD])
        na, nb = qh[:, :LANES], qh[:, LANES:2 * LANES]
        xr, xp = qh[:, 2 * LANES:3 * LANES], qh[:, 3 * LANES:]
        sq = xr * xr
        ss_a = jnp.sum(na * na + jnp.where(head_a, sq, 0.0), axis=-1, keepdims=True)
        ss_b = jnp.sum(nb * nb + jnp.where(head_a, 0.0, sq), axis=-1, keepdims=True)
        r_a = lax.rsqrt(ss_a * (1.0 / QK_HEAD) + NORM_EPS) * ATTN_SCALE
        r_b = lax.rsqrt(ss_b * (1.0 / QK_HEAD) + NORM_EPS) * ATTN_SCALE
        rot = ((xr * gc + xp * gs) * jnp.where(head_a, r_a, r_b)).astype(BF16)
        lo = pair * 2 * HEAD_PAD
        q_ref[:, lo:lo + LANES] = (na * r_a * gn).astype(BF16)
        q_ref[:, lo + LANES:lo + HEAD_PAD] = rot
        q_ref[:, lo + HEAD_PAD:lo + HEAD_PAD + LANES] = (nb * r_b * gn).astype(BF16)
        q_ref[:, lo + HEAD_PAD + LANES:lo + 2 * HEAD_PAD] = rot


def _q_proj(h, g, wd, lg, wu, gn, g2, g3, cm, sm):
    t = h.shape[0]
    rows = lambda width: _rows(width, PROJ_TILE)
    return pl.pallas_call(
        _q_proj_kernel,
        grid=(t // PROJ_TILE,),
        in_specs=[rows(D_MODEL), _resident((1, D_MODEL)), _resident((D_MODEL, Q_LORA)),
                  _resident((1, Q_LORA)), _resident((Q_LORA, MLA_HEADS * HEAD_PAD)),
                  _resident((1, LANES)), _resident((1, LANES)), _resident((1, LANES)),
                  rows(LANES), rows(LANES)],
        out_specs=rows(MLA_HEADS * HEAD_PAD),
        out_shape=jax.ShapeDtypeStruct((t, MLA_HEADS * HEAD_PAD), BF16),
        compiler_params=_cparams("parallel"),
        name="q_proj",
    )(h, g, wd, lg, wu, gn, g2, g3, cm, sm)


def _pair_up_proj(w):
    w = w.reshape(Q_LORA, MLA_HEADS // 2, 2, QK_HEAD)
    nope = w[..., :QK_NOPE]
    x1 = w[..., QK_NOPE:QK_NOPE + ROPE_HALF]
    x2 = w[..., QK_NOPE + ROPE_HALF:]
    a, b = 0, 1
    slab = jnp.concatenate(
        [nope[:, :, a], nope[:, :, b],
         x1[:, :, a], x2[:, :, a], x1[:, :, b], x2[:, :, b],
         x2[:, :, a], x1[:, :, a], x2[:, :, b], x1[:, :, b]], axis=-1)
    return slab.reshape(Q_LORA, MLA_HEADS * HEAD_PAD)


def _attn_kernel(q_ref, k_ref, v_ref, o_ref, vaug_ref):
    seq = k_ref.shape[0]
    tq = ATTN_TQ
    vaug_ref[:, :V_HEAD] = v_ref[...]
    vaug_ref[:, V_HEAD:] = jnp.ones((seq, HEAD_PAD - V_HEAD), BF16)
    row = lax.broadcasted_iota(jnp.int32, (tq, tq), 0)
    col = lax.broadcasted_iota(jnp.int32, (tq, tq), 1)
    causal = col <= row
    for qi in reversed(range(seq // tq)):
        n0 = qi * tq
        q = q_ref[n0:n0 + tq, :]
        s_diag = jnp.where(causal, _dot_nt(q, k_ref[n0:n0 + tq, :]), -jnp.inf)
        m = jnp.max(s_diag, axis=-1, keepdims=True)
        if qi > 0:
            s_off = _dot_nt(q, k_ref[:n0, :])
            m = jnp.maximum(m, jnp.max(s_off, axis=-1, keepdims=True))
            p = jnp.concatenate([jnp.exp2(s_off - m), jnp.exp2(s_diag - m)], axis=1)
        else:
            p = jnp.exp2(s_diag - m)
        oa = _dot(p.astype(BF16), vaug_ref[:n0 + tq, :])
        o_ref[n0:n0 + tq, :] = (oa[:, :V_HEAD] / oa[:, V_HEAD:]).astype(BF16)


def _attention(q, k, v, batch, seq):
    q3 = q.reshape(batch, seq, MLA_HEADS * HEAD_PAD)
    k3 = k.reshape(batch, seq, MLA_HEADS * HEAD_PAD)
    v3 = v.reshape(batch, seq, MLA_HEADS * V_HEAD)
    qk_spec = pl.BlockSpec((None, seq, HEAD_PAD), lambda b, h: (b, 0, h))
    v_spec = pl.BlockSpec((None, seq, V_HEAD), lambda b, h: (b, 0, h))
    out = pl.pallas_call(
        _attn_kernel,
        grid=(batch, MLA_HEADS),
        in_specs=[qk_spec, qk_spec, v_spec],
        out_specs=v_spec,
        out_shape=jax.ShapeDtypeStruct((batch, seq, MLA_HEADS * V_HEAD), BF16),
        scratch_shapes=[pltpu.VMEM((seq, HEAD_PAD), BF16)],
        compiler_params=_cparams("parallel", "parallel"),
        name="attention",
    )(q3, k3, v3)
    return out.reshape(batch * seq, MLA_HEADS * V_HEAD)


def _pad_lanes(a, width):
    return jnp.pad(a, [(0, 0)] * (a.ndim - 1) + [(0, width - a.shape[-1])])


def kernel(x, positions, attn_norm_g, ffn_norm_g, ffn_w_in, ffn_w_out, ret_w_in, ret_w_out,
           kv_norm_g, w_dkv, ckv_norm_g, w_uk, w_uv, k_norm_g,
           mla_w_dq, q_lora_norm_g, mla_w_uq, q_norm_g, mla_w_o):
    batch, seq, _ = x.shape
    t = batch * seq
    row = lambda a: a.reshape(1, -1)

    cr, sr, cm, sm = _rope_tables(positions)
    log_gamma = jnp.log(1.0 - 2.0 ** (-5.0 - jnp.arange(RET_HEADS, dtype=F32)))

    h = x.reshape(t, D_MODEL)
    for l in range(DEPTH):
        if l < N_RET_LAYERS:
            q, k, v, gs = _ret_proj(log_gamma, h, row(attn_norm_g[l]), ret_w_in[l].astype(BF16), cr, sr)
            y = _retention(log_gamma, q, k, v, gs, batch, seq)
            w_mix = ret_w_out[l]
        else:
            j = l - N_RET_LAYERS
            gx1 = q_norm_g[j][QK_NOPE:QK_NOPE + ROPE_HALF]
            gx2 = q_norm_g[j][QK_NOPE + ROPE_HALF:]
            q = _q_proj(h, row(attn_norm_g[l]), mla_w_dq[j].astype(BF16), row(q_lora_norm_g[j]),
                        _pair_up_proj(mla_w_uq[j]).astype(BF16), row(q_norm_g[j][:QK_NOPE]),
                        row(jnp.concatenate([gx1, gx2, gx1, gx2])), row(jnp.concatenate([gx2, gx1, gx2, gx1])),
                        cm, sm)
            y = _attention(q, shared_k, shared_v, batch, seq)
            w_mix = mla_w_o[j]
        h = _mix_ffn(y, w_mix.astype(BF16), h, row(ffn_norm_g[l]),
                     ffn_w_in[l].astype(BF16), ffn_w_out[l].astype(BF16))
        if l == N_RET_LAYERS - 1:
            shared_k, shared_v = _shared_kv(
                h, row(kv_norm_g), _pad_lanes(w_dkv, KV_LORA + LANES).astype(BF16), row(ckv_norm_g),
                w_uk.astype(BF16), w_uv.astype(BF16),
                row(k_norm_g[:QK_NOPE]), row(_pad_lanes(k_norm_g[QK_NOPE:], LANES)), cm, sm)
    return h.reshape(batch, seq, D_MODEL)
```

```python
import math

import jax
import jax.numpy as jnp
from jax import lax
from jax.experimental import pallas as pl
from jax.experimental.pallas import tpu as pltpu

F32 = jnp.float32
BF16 = jnp.bfloat16

D_MODEL = 1024
DEPTH = 4
N_RET_LAYERS = DEPTH // 2
RET_HEADS = 4
RET_DK = 256
RET_DV = 512
RET_QK = RET_HEADS * RET_DK
RET_V = RET_HEADS * RET_DV
RET_IN = 2 * RET_QK + 2 * RET_V
MLA_HEADS = 16
QK_NOPE = 128
QK_ROPE = 64
QK_HEAD = QK_NOPE + QK_ROPE
ROPE_HALF = QK_ROPE // 2
V_HEAD = 128
Q_LORA = 384
KV_LORA = 256
FFN_HIDDEN = 2816
ROPE_BASE = 10000.0
NORM_EPS = 1e-6
ATTN_SCALE = math.log2(math.e) / math.sqrt(QK_HEAD)

LANES = 128
MXU_DIM = 256
HEAD_PAD = 2 * LANES
VMEM_LIMIT_BYTES = 56 * 1024 * 1024

ROW_TILE = 1024
FFN_CHUNK = MXU_DIM
RET_CHUNK = 256
RET_ROWS = 1024
ATTN_TQ = 512
PROJ_TILE = 1024


def _cparams(*sem):
    return pltpu.CompilerParams(dimension_semantics=sem, vmem_limit_bytes=VMEM_LIMIT_BYTES)


def _resident(shape):
    zeros = (0,) * len(shape)
    return pl.BlockSpec(shape, lambda *_: zeros, pipeline_mode=pl.Buffered(1))


def _rows(width, tile=ROW_TILE):
    return pl.BlockSpec((tile, width), lambda i: (i, 0))


def _rms(x):
    return x * lax.rsqrt(jnp.mean(x * x, axis=-1, keepdims=True) + NORM_EPS)


def _dot(a, b):
    return jnp.dot(a, b, preferred_element_type=F32)


def _dot_nt(a, b):
    return lax.dot_general(a, b, (((1,), (1,)), ((), ())), preferred_element_type=F32)


def _dot_tn(a, b):
    return lax.dot_general(a, b, (((0,), (0,)), ((), ())), preferred_element_type=F32)


def _select_lanes(x, sel):
    hi = x.astype(BF16)
    rest = x - hi.astype(F32)
    mid = rest.astype(BF16)
    lo = (rest - mid.astype(F32)).astype(BF16)
    return (_dot(hi, sel) + _dot(mid, sel)) + _dot(lo, sel)


def _rope_table_kernel(pos_ref, fr_ref, sel_ref, sg_ref, cr_ref, sr_ref, cm_ref, sm_ref):
    p = pos_ref[...].astype(F32)
    ar = p * fr_ref[...]
    c = jnp.cos(ar)
    s = jnp.sin(ar)
    cr_ref[...] = c
    sr_ref[...] = s
    sel = sel_ref[...]
    cm_ref[...] = _select_lanes(c, sel)
    sm_ref[...] = _select_lanes(s, sel) * sg_ref[...]


def _rope_tables(positions):
    t = positions.size
    pos = positions.reshape(t, 1)
    fr = ROPE_BASE ** (-jnp.arange(0, RET_DK, 2, dtype=F32) / RET_DK)
    stride = RET_DK // QK_ROPE
    src = stride * (jnp.arange(LANES) % ROPE_HALF)
    sel = (jnp.arange(LANES)[:, None] == src[None, :]).astype(BF16)
    sg = jnp.tile(jnp.concatenate([-jnp.ones((ROPE_HALF,), F32), jnp.ones((ROPE_HALF,), F32)]),
                  LANES // QK_ROPE)
    row = lambda a: a.reshape(1, LANES)
    assert RET_DK // 2 == LANES
    tab = jax.ShapeDtypeStruct((t, LANES), F32)
    return pl.pallas_call(
        _rope_table_kernel,
        grid=(t // ROW_TILE,),
        in_specs=[_rows(1), _resident((1, LANES)), _resident((LANES, LANES)), _resident((1, LANES))],
        out_specs=[_rows(LANES)] * 4,
        out_shape=[tab] * 4,
        compiler_params=_cparams("parallel"),
        name="rope_tables",
    )(pos, row(fr), sel, row(sg))


def _lane_mask(lo, hi):
    lane = lax.broadcasted_iota(jnp.int32, (1, LANES), 1)
    return jnp.logical_and(lane >= lo, lane < hi)


def _ret_proj_kernel(lg_ref, h_ref, g_ref, w_ref, c_ref, s_ref, q_ref, k_ref, v_ref, gs_ref):
    hn = (_rms(h_ref[...]) * g_ref[...]).astype(BF16)
    c = c_ref[...]
    s = s_ref[...]
    half = RET_DK // 2
    in_chunk = jnp.bitwise_and(lax.broadcasted_iota(jnp.int32, (PROJ_TILE, 1), 0), RET_CHUNK - 1)
    back = (RET_CHUNK - 1 - in_chunk).astype(F32)
    for dst, base in ((q_ref, 0), (k_ref, RET_QK)):
        for hd in range(RET_HEADS):
            lo = hd * RET_DK
            p = _dot(hn, w_ref[:, base + lo:base + lo + RET_DK])
            x1, x2 = p[:, :half], p[:, half:]
            scale = 1.0 if dst is q_ref else jnp.exp(back * lg_ref[hd]) * RET_DK ** -0.5
            dst[:, lo:lo + half] = ((x1 * c - x2 * s) * scale).astype(BF16)
            dst[:, lo + half:lo + RET_DK] = ((x2 * c + x1 * s) * scale).astype(BF16)
    for hd in range(RET_HEADS):
        lo = hd * RET_DV
        v_ref[:, lo:lo + RET_DV] = _dot(hn, w_ref[:, 2 * RET_QK + lo:2 * RET_QK + lo + RET_DV]).astype(BF16)
    for hd in range(RET_HEADS):
        lo = hd * RET_DV
        gate = _dot(hn, w_ref[:, 2 * RET_QK + RET_V + lo:2 * RET_QK + RET_V + lo + RET_DV])
        gs_ref[:, lo:lo + RET_DV] = (gate * jax.nn.sigmoid(gate)).astype(BF16)


def _ret_proj(log_gamma, h, g, w, cr, sr):
    t = h.shape[0]
    assert PROJ_TILE % RET_CHUNK == 0
    rows = lambda width: _rows(width, PROJ_TILE)
    return pl.pallas_call(
        _ret_proj_kernel,
        grid=(t // PROJ_TILE,),
        in_specs=[pl.BlockSpec(memory_space=pltpu.SMEM),
                  rows(D_MODEL), _resident((1, D_MODEL)), _resident((D_MODEL, RET_IN)),
                  rows(LANES), rows(LANES)],
        out_specs=[rows(RET_QK), rows(RET_QK), rows(RET_V), rows(RET_V)],
        out_shape=[jax.ShapeDtypeStruct((t, RET_QK), BF16), jax.ShapeDtypeStruct((t, RET_QK), BF16),
                   jax.ShapeDtypeStruct((t, RET_V), BF16), jax.ShapeDtypeStruct((t, RET_V), BF16)],
        compiler_params=_cparams("parallel"),
        name="ret_proj",
    )(log_gamma, h, g, w, cr, sr)


def _retention_kernel(lg_ref, q_ref, k_ref, v_ref, gs_ref, o_ref, state_ref):
    c = RET_CHUNK

    @pl.when(pl.program_id(1) == 0)
    def _():
        state_ref[...] = jnp.zeros_like(state_ref)

    ii = lax.broadcasted_iota(jnp.int32, (c, c), 0)
    jj = lax.broadcasted_iota(jnp.int32, (c, c), 1)
    lower = jj <= ii
    back = (c - 1 - lax.broadcasted_iota(jnp.int32, (c, 1), 0)).astype(F32)

    consts = []
    for hd in range(RET_HEADS):
        lg = lg_ref[hd]
        inv_rho = jnp.exp(back * lg)
        consts.append((NORM_EPS * inv_rho * inv_rho,
                       jnp.exp(jnp.full((1, RET_DV), float(c), F32) * lg)))

    for ci in range(RET_ROWS // c):
        rows = slice(ci * c, (ci + 1) * c)
        for hd in range(RET_HEADS):
            eps_rows, chunk_decay = consts[hd]
            qk_cols = slice(hd * RET_DK, (hd + 1) * RET_DK)
            v_cols = slice(hd * RET_DV, (hd + 1) * RET_DV)
            qc = q_ref[rows, qk_cols]
            kc = k_ref[rows, qk_cols]
            vc = v_ref[rows, v_cols]
            st = state_ref[hd] * chunk_decay
            p = jnp.where(lower, _dot_nt(qc, kc), 0.0).astype(BF16)
            o = _dot(qc, st.astype(BF16)) + _dot(p, vc)
            state_ref[hd] = st + _dot_tn(kc, vc)
            y = o * lax.rsqrt(jnp.mean(o * o, axis=-1, keepdims=True) + eps_rows)
            o_ref[rows, v_cols] = y.astype(BF16) * gs_ref[rows, v_cols]


def _retention(log_gamma, q, k, v, gs, batch, seq):
    q3 = q.reshape(batch, seq, RET_QK)
    k3 = k.reshape(batch, seq, RET_QK)
    v3 = v.reshape(batch, seq, RET_V)
    g3 = gs.reshape(batch, seq, RET_V)
    qk_spec = pl.BlockSpec((None, RET_ROWS, RET_QK), lambda b, s: (b, s, 0))
    v_spec = pl.BlockSpec((None, RET_ROWS, RET_V), lambda b, s: (b, s, 0))
    out = pl.pallas_call(
        _retention_kernel,
        grid=(batch, seq // RET_ROWS),
        in_specs=[pl.BlockSpec(memory_space=pltpu.SMEM), qk_spec, qk_spec, v_spec, v_spec],
        out_specs=v_spec,
        out_shape=jax.ShapeDtypeStruct((batch, seq, RET_V), BF16),
        scratch_shapes=[pltpu.VMEM((RET_HEADS, RET_DK, RET_DV), F32)],
        compiler_params=_cparams("parallel", "arbitrary"),
        name="retention",
    )(log_gamma, q3, k3, v3, g3)
    return out.reshape(batch * seq, RET_V)


def _mix_ffn_kernel(y_ref, wm_ref, h_ref, g_ref, wi_ref, wo_ref, o_ref, act_ref):
    x = h_ref[...] + _dot(y_ref[...], wm_ref[...])
    hn = (_rms(x) * g_ref[...]).astype(BF16)
    for f in range(FFN_HIDDEN // FFN_CHUNK):
        lo = f * FFN_CHUNK
        a = _dot(hn, wi_ref[:, lo:lo + FFN_CHUNK])
        b = _dot(hn, wi_ref[:, FFN_HIDDEN + lo:FFN_HIDDEN + lo + FFN_CHUNK])
        act_ref[:, lo:lo + FFN_CHUNK] = (a * jax.nn.sigmoid(a) * b).astype(BF16)
    o_ref[...] = x + _dot(act_ref[...], wo_ref[...])


def _mix_ffn(y, w_mix, h, g, w_in, w_out):
    t, kdim = y.shape
    return pl.pallas_call(
        _mix_ffn_kernel,
        grid=(t // ROW_TILE,),
        in_specs=[_rows(kdim), _resident((kdim, D_MODEL)), _rows(D_MODEL), _resident((1, D_MODEL)),
                  _resident((D_MODEL, 2 * FFN_HIDDEN)), _resident((FFN_HIDDEN, D_MODEL))],
        out_specs=_rows(D_MODEL),
        out_shape=jax.ShapeDtypeStruct((t, D_MODEL), F32),
        scratch_shapes=[pltpu.VMEM((ROW_TILE, FFN_HIDDEN), BF16)],
        compiler_params=_cparams("parallel"),
        name="mix_ffn",
    )(y, w_mix, h, g, w_in, w_out)


def _kv_kernel(h_ref, g_ref, wd_ref, cg_ref, wuk_ref, wuv_ref, gn_ref, gr_ref,
               cm_ref, sm_ref, k_ref, v_ref):
    hn = (_rms(h_ref[...]) * g_ref[...]).astype(BF16)
    ckr = _dot(hn, wd_ref[...])
    c = (_rms(ckr[:, :KV_LORA]) * cg_ref[...]).astype(BF16)
    kr = ckr[:, KV_LORA:]
    kr_ss = jnp.sum(kr * kr, axis=-1, keepdims=True)
    krg = kr * gr_ref[...]
    sm = sm_ref[...]
    sa = jnp.where(_lane_mask(0, ROPE_HALF), sm, 0.0)
    sb = jnp.where(_lane_mask(ROPE_HALF, QK_ROPE), sm, 0.0)
    rot_a = krg * cm_ref[...] + pltpu.roll(krg, LANES - ROPE_HALF, 1) * sa + pltpu.roll(krg, ROPE_HALF, 1) * sb
    rot_b = pltpu.roll(rot_a, QK_ROPE, 1)
    gn = gn_ref[...]
    for hd in range(MLA_HEADS):
        kn = _dot(c, wuk_ref[:, hd * QK_NOPE:(hd + 1) * QK_NOPE])
        r = lax.rsqrt((jnp.sum(kn * kn, axis=-1, keepdims=True) + kr_ss) * (1.0 / QK_HEAD) + NORM_EPS)
        k_ref[:, hd * HEAD_PAD:hd * HEAD_PAD + LANES] = (kn * r * gn).astype(BF16)
        rot = rot_a if hd % 2 == 0 else rot_b
        k_ref[:, hd * HEAD_PAD + LANES:(hd + 1) * HEAD_PAD] = (rot * r).astype(BF16)
    v_ref[...] = _dot(c, wuv_ref[...]).astype(BF16)


def _shared_kv(h, g, wd, cg, wuk, wuv, gn, gr, cm, sm):
    t = h.shape[0]
    rows = lambda width: _rows(width, PROJ_TILE)
    return pl.pallas_call(
        _kv_kernel,
        grid=(t // PROJ_TILE,),
        in_specs=[rows(D_MODEL), _resident((1, D_MODEL)), _resident((D_MODEL, KV_LORA + LANES)),
                  _resident((1, KV_LORA)), _resident((KV_LORA, MLA_HEADS * QK_NOPE)),
                  _resident((KV_LORA, MLA_HEADS * V_HEAD)), _resident((1, LANES)), _resident((1, LANES)),
                  rows(LANES), rows(LANES)],
        out_specs=[rows(MLA_HEADS * HEAD_PAD), rows(MLA_HEADS * V_HEAD)],
        out_shape=[jax.ShapeDtypeStruct((t, MLA_HEADS * HEAD_PAD), BF16),
                   jax.ShapeDtypeStruct((t, MLA_HEADS * V_HEAD), BF16)],
        compiler_params=_cparams("parallel"),
        name="shared_kv",
    )(h, g, wd, cg, wuk, wuv, gn, gr, cm, sm)


def _q_proj_kernel(h_ref, g_ref, wd_ref, lg_ref, wu_ref, gn_ref, g2_ref, g3_ref, cm_ref, sm_ref, q_ref):
    hn = (_rms(h_ref[...]) * g_ref[...]).astype(BF16)
    cq = (_rms(_dot(hn, wd_ref[...])) * lg_ref[...]).astype(BF16)
    gn = gn_ref[...]
    gc = g2_ref[...] * cm_ref[...]
    gs = g3_ref[...] * sm_ref[...]
    head_a = _lane_mask(0, QK_ROPE)
    for pair in range(MLA_HEADS // 2):
        qh = _dot(cq, wu_ref[:, pair * 2 * HEAD_PAD:(pair + 1) * 2 * HEAD_PAD])
        na, nb = qh[:, :LANES], qh[:, LANES:2 * LANES]
        xr, xp = qh[:, 2 * LANES:3 * LANES], qh[:, 3 * LANES:]
        sq = xr * xr
        ss_a = jnp.sum(na * na + jnp.where(head_a, sq, 0.0), axis=-1, keepdims=True)
        ss_b = jnp.sum(nb * nb + jnp.where(head_a, 0.0, sq), axis=-1, keepdims=True)
        r_a = lax.rsqrt(ss_a * (1.0 / QK_HEAD) + NORM_EPS) * ATTN_SCALE
        r_b = lax.rsqrt(ss_b * (1.0 / QK_HEAD) + NORM_EPS) * ATTN_SCALE
        rot = ((xr * gc + xp * gs) * jnp.where(head_a, r_a, r_b)).astype(BF16)
        lo = pair * 2 * HEAD_PAD
        q_ref[:, lo:lo + LANES] = (na * r_a * gn).astype(BF16)
        q_ref[:, lo + LANES:lo + HEAD_PAD] = rot
        q_ref[:, lo + HEAD_PAD:lo + HEAD_PAD + LANES] = (nb * r_b * gn).astype(BF16)
        q_ref[:, lo + HEAD_PAD + LANES:lo + 2 * HEAD_PAD] = rot


def _q_proj(h, g, wd, lg, wu, gn, g2, g3, cm, sm):
    t = h.shape[0]
    rows = lambda width: _rows(width, PROJ_TILE)
    return pl.pallas_call(
        _q_proj_kernel,
        grid=(t // PROJ_TILE,),
        in_specs=[rows(D_MODEL), _resident((1, D_MODEL)), _resident((D_MODEL, Q_LORA)),
                  _resident((1, Q_LORA)), _resident((Q_LORA, MLA_HEADS * HEAD_PAD)),
                  _resident((1, LANES)), _resident((1, LANES)), _resident((1, LANES)),
                  rows(LANES), rows(LANES)],
        out_specs=rows(MLA_HEADS * HEAD_PAD),
        out_shape=jax.ShapeDtypeStruct((t, MLA_HEADS * HEAD_PAD), BF16),
        compiler_params=_cparams("parallel"),
        name="q_proj",
    )(h, g, wd, lg, wu, gn, g2, g3, cm, sm)


def _pair_up_proj(w):
    w = w.reshape(Q_LORA, MLA_HEADS // 2, 2, QK_HEAD)
    nope = w[..., :QK_NOPE]
    x1 = w[..., QK_NOPE:QK_NOPE + ROPE_HALF]
    x2 = w[..., QK_NOPE + ROPE_HALF:]
    a, b = 0, 1
    slab = jnp.concatenate(
        [nope[:, :, a], nope[:, :, b],
         x1[:, :, a], x2[:, :, a], x1[:, :, b], x2[:, :, b],
         x2[:, :, a], x1[:, :, a], x2[:, :, b], x1[:, :, b]], axis=-1)
    return slab.reshape(Q_LORA, MLA_HEADS * HEAD_PAD)


def _attn_kernel(q_ref, k_ref, v_ref, o_ref, vaug_ref):
    seq = k_ref.shape[0]
    tq = ATTN_TQ
    vaug_ref[:, :V_HEAD] = v_ref[...]
    vaug_ref[:, V_HEAD:] = jnp.ones((seq, HEAD_PAD - V_HEAD), BF16)
    row = lax.broadcasted_iota(jnp.int32, (tq, tq), 0)
    col = lax.broadcasted_iota(jnp.int32, (tq, tq), 1)
    causal = col <= row
    for qi in reversed(range(seq // tq)):
        n0 = qi * tq
        q = q_ref[n0:n0 + tq, :]
        s_diag = jnp.where(causal, _dot_nt(q, k_ref[n0:n0 + tq, :]), -jnp.inf)
        m = jnp.max(s_diag, axis=-1, keepdims=True)
        if qi > 0:
            s_off = _dot_nt(q, k_ref[:n0, :])
            m = jnp.maximum(m, jnp.max(s_off, axis=-1, keepdims=True))
            p = jnp.concatenate([jnp.exp2(s_off - m), jnp.exp2(s_diag - m)], axis=1)
        else:
            p = jnp.exp2(s_diag - m)
        oa = _dot(p.astype(BF16), vaug_ref[:n0 + tq, :])
        o_ref[n0:n0 + tq, :] = (oa[:, :V_HEAD] / oa[:, V_HEAD:]).astype(BF16)


def _attention(q, k, v, batch, seq):
    q3 = q.reshape(batch, seq, MLA_HEADS * HEAD_PAD)
    k3 = k.reshape(batch, seq, MLA_HEADS * HEAD_PAD)
    v3 = v.reshape(batch, seq, MLA_HEADS * V_HEAD)
    qk_spec = pl.BlockSpec((None, seq, HEAD_PAD), lambda b, h: (b, 0, h))
    v_spec = pl.BlockSpec((None, seq, V_HEAD), lambda b, h: (b, 0, h))
    out = pl.pallas_call(
        _attn_kernel,
        grid=(batch, MLA_HEADS),
        in_specs=[qk_spec, qk_spec, v_spec],
        out_specs=v_spec,
        out_shape=jax.ShapeDtypeStruct((batch, seq, MLA_HEADS * V_HEAD), BF16),
        scratch_shapes=[pltpu.VMEM((seq, HEAD_PAD), BF16)],
        compiler_params=_cparams("parallel", "parallel"),
        name="attention",
    )(q3, k3, v3)
    return out.reshape(batch * seq, MLA_HEADS * V_HEAD)


def _pad_lanes(a, width):
    return jnp.pad(a, [(0, 0)] * (a.ndim - 1) + [(0, width - a.shape[-1])])


def kernel(x, positions, attn_norm_g, ffn_norm_g, ffn_w_in, ffn_w_out, ret_w_in, ret_w_out,
           kv_norm_g, w_dkv, ckv_norm_g, w_uk, w_uv, k_norm_g,
           mla_w_dq, q_lora_norm_g, mla_w_uq, q_norm_g, mla_w_o):
    batch, seq, _ = x.shape
    t = batch * seq
    row = lambda a: a.reshape(1, -1)

    cr, sr, cm, sm = _rope_tables(positions)
    log_gamma = jnp.log(1.0 - 2.0 ** (-5.0 - jnp.arange(RET_HEADS, dtype=F32)))

    h = x.reshape(t, D_MODEL)
    for l in range(DEPTH):
        if l < N_RET_LAYERS:
            q, k, v, gs = _ret_proj(log_gamma, h, row(attn_norm_g[l]), ret_w_in[l].astype(BF16), cr, sr)
            y = _retention(log_gamma, q, k, v, gs, batch, seq)
            w_mix = ret_w_out[l]
        else:
            j = l - N_RET_LAYERS
            gx1 = q_norm_g[j][QK_NOPE:QK_NOPE + ROPE_HALF]
            gx2 = q_norm_g[j][QK_NOPE + ROPE_HALF:]
            q = _q_proj(h, row(attn_norm_g[l]), mla_w_dq[j].astype(BF16), row(q_lora_norm_g[j]),
                        _pair_up_proj(mla_w_uq[j]).astype(BF16), row(q_norm_g[j][:QK_NOPE]),
                        row(jnp.concatenate([gx1, gx2, gx1, gx2])), row(jnp.concatenate([gx2, gx1, gx2, gx1])),
                        cm, sm)
            y = _attention(q, shared_k, shared_v, batch, seq)
            w_mix = mla_w_o[j]
        h = _mix_ffn(y, w_mix.astype(BF16), h, row(ffn_norm_g[l]),
                     ffn_w_in[l].astype(BF16), ffn_w_out[l].astype(BF16))
        if l == N_RET_LAYERS - 1:
            shared_k, shared_v = _shared_kv(
                h, row(kv_norm_g), _pad_lanes(w_dkv, KV_LORA + LANES).astype(BF16), row(ckv_norm_g),
                w_uk.astype(BF16), w_uv.astype(BF16),
                row(k_norm_g[:QK_NOPE]), row(_pad_lanes(k_norm_g[QK_NOPE:], LANES)), cm, sm)
    return h.reshape(batch, seq, D_MODEL)
```

```python
import math

import jax
import jax.numpy as jnp
from jax import lax
from jax.experimental import pallas as pl
from jax.experimental.pallas import tpu as pltpu

F32 = jnp.float32
BF16 = jnp.bfloat16

D_MODEL = 1024
DEPTH = 4
N_RET_LAYERS = DEPTH // 2
RET_HEADS = 4
RET_DK = 256
RET_DV = 512
RET_QK = RET_HEADS * RET_DK
RET_V = RET_HEADS * RET_DV
RET_IN = 2 * RET_QK + 2 * RET_V
MLA_HEADS = 16
QK_NOPE = 128
QK_ROPE = 64
QK_HEAD = QK_NOPE + QK_ROPE
ROPE_HALF = QK_ROPE // 2
V_HEAD = 128
Q_LORA = 384
KV_LORA = 256
FFN_HIDDEN = 2816
ROPE_BASE = 10000.0
NORM_EPS = 1e-6
ATTN_SCALE = math.log2(math.e) / math.sqrt(QK_HEAD)

LANES = 128
MXU_DIM = 256
HEAD_PAD = 2 * LANES
VMEM_LIMIT_BYTES = 56 * 1024 * 1024

ROW_TILE = 1024
FFN_CHUNK = MXU_DIM
RET_CHUNK = 256
RET_ROWS = 1024
ATTN_TQ = 512
PROJ_TILE = 1024


def _cparams(*sem):
    return pltpu.CompilerParams(dimension_semantics=sem, vmem_limit_bytes=VMEM_LIMIT_BYTES)


def _resident(shape):
    zeros = (0,) * len(shape)
    return pl.BlockSpec(shape, lambda *_: zeros, pipeline_mode=pl.Buffered(1))


def _rows(width, tile=ROW_TILE):
    return pl.BlockSpec((tile, width), lambda i: (i, 0))


def _rms(x):
    return x * lax.rsqrt(jnp.mean(x * x, axis=-1, keepdims=True) + NORM_EPS)


def _dot(a, b):
    return jnp.dot(a, b, preferred_element_type=F32)


def _dot_nt(a, b):
    return lax.dot_general(a, b, (((1,), (1,)), ((), ())), preferred_element_type=F32)


def _dot_tn(a, b):
    return lax.dot_general(a, b, (((0,), (0,)), ((), ())), preferred_element_type=F32)


def _select_lanes(x, sel):
    hi = x.astype(BF16)
    rest = x - hi.astype(F32)
    mid = rest.astype(BF16)
    lo = (rest - mid.astype(F32)).astype(BF16)
    return (_dot(hi, sel) + _dot(mid, sel)) + _dot(lo, sel)


def _rope_table_kernel(pos_ref, fr_ref, sel_ref, sg_ref, cr_ref, sr_ref, cm_ref, sm_ref):
    p = pos_ref[...].astype(F32)
    ar = p * fr_ref[...]
    c = jnp.cos(ar)
    s = jnp.sin(ar)
    cr_ref[...] = c
    sr_ref[...] = s
    sel = sel_ref[...]
    cm_ref[...] = _select_lanes(c, sel)
    sm_ref[...] = _select_lanes(s, sel) * sg_ref[...]


def _rope_tables(positions):
    t = positions.size
    pos = positions.reshape(t, 1)
    fr = ROPE_BASE ** (-jnp.arange(0, RET_DK, 2, dtype=F32) / RET_DK)
    stride = RET_DK // QK_ROPE
    src = stride * (jnp.arange(LANES) % ROPE_HALF)
    sel = (jnp.arange(LANES)[:, None] == src[None, :]).astype(BF16)
    sg = jnp.tile(jnp.concatenate([-jnp.ones((ROPE_HALF,), F32), jnp.ones((ROPE_HALF,), F32)]),
                  LANES // QK_ROPE)
    row = lambda a: a.reshape(1, LANES)
    assert RET_DK // 2 == LANES
    tab = jax.ShapeDtypeStruct((t, LANES), F32)
    return pl.pallas_call(
        _rope_table_kernel,
        grid=(t // ROW_TILE,),
        in_specs=[_rows(1), _resident((1, LANES)), _resident((LANES, LANES)), _resident((1, LANES))],
        out_specs=[_rows(LANES)] * 4,
        out_shape=[tab] * 4,
        compiler_params=_cparams("parallel"),
        name="rope_tables",
    )(pos, row(fr), sel, row(sg))


def _lane_mask(lo, hi):
    lane = lax.broadcasted_iota(jnp.int32, (1, LANES), 1)
    return jnp.logical_and(lane >= lo, lane < hi)


def _ret_proj_kernel(lg_ref, h_ref, g_ref, w_ref, c_ref, s_ref, q_ref, k_ref, v_ref, gs_ref):
    hn = (_rms(h_ref[...]) * g_ref[...]).astype(BF16)
    c = c_ref[...]
    s = s_ref[...]
    half = RET_DK // 2
    in_chunk = jnp.bitwise_and(lax.broadcasted_iota(jnp.int32, (PROJ_TILE, 1), 0), RET_CHUNK - 1)
    back = (RET_CHUNK - 1 - in_chunk).astype(F32)
    for dst, base in ((q_ref, 0), (k_ref, RET_QK)):
        for hd in range(RET_HEADS):
            lo = hd * RET_DK
            p = _dot(hn, w_ref[:, base + lo:base + lo + RET_DK])
            x1, x2 = p[:, :half], p[:, half:]
            scale = 1.0 if dst is q_ref else jnp.exp(back * lg_ref[hd]) * RET_DK ** -0.5
            dst[:, lo:lo + half] = ((x1 * c - x2 * s) * scale).astype(BF16)
            dst[:, lo + half:lo + RET_DK] = ((x2 * c + x1 * s) * scale).astype(BF16)
    for hd in range(RET_HEADS):
        lo = hd * RET_DV
        v_ref[:, lo:lo + RET_DV] = _dot(hn, w_ref[:, 2 * RET_QK + lo:2 * RET_QK + lo + RET_DV]).astype(BF16)
    for hd in range(RET_HEADS):
        lo = hd * RET_DV
        gate = _dot(hn, w_ref[:, 2 * RET_QK + RET_V + lo:2 * RET_QK + RET_V + lo + RET_DV])
        gs_ref[:, lo:lo + RET_DV] = (gate * jax.nn.sigmoid(gate)).astype(BF16)


def _ret_proj(log_gamma, h, g, w, cr, sr):
    t = h.shape[0]
    assert PROJ_TILE % RET_CHUNK == 0
    rows = lambda width: _rows(width, PROJ_TILE)
    return pl.pallas_call(
        _ret_proj_kernel,
        grid=(t // PROJ_TILE,),
        in_specs=[pl.BlockSpec(memory_space=pltpu.SMEM),
                  rows(D_MODEL), _resident((1, D_MODEL)), _resident((D_MODEL, RET_IN)),
                  rows(LANES), rows(LANES)],
        out_specs=[rows(RET_QK), rows(RET_QK), rows(RET_V), rows(RET_V)],
        out_shape=[jax.ShapeDtypeStruct((t, RET_QK), BF16), jax.ShapeDtypeStruct((t, RET_QK), BF16),
                   jax.ShapeDtypeStruct((t, RET_V), BF16), jax.ShapeDtypeStruct((t, RET_V), BF16)],
        compiler_params=_cparams("parallel"),
        name="ret_proj",
    )(log_gamma, h, g, w, cr, sr)


def _retention_kernel(lg_ref, q_ref, k_ref, v_ref, gs_ref, o_ref, state_ref):
    c = RET_CHUNK

    @pl.when(pl.program_id(1) == 0)
    def _():
        state_ref[...] = jnp.zeros_like(state_ref)

    ii = lax.broadcasted_iota(jnp.int32, (c, c), 0)
    jj = lax.broadcasted_iota(jnp.int32, (c, c), 1)
    lower = jj <= ii
    back = (c - 1 - lax.broadcasted_iota(jnp.int32, (c, 1), 0)).astype(F32)

    consts = []
    for hd in range(RET_HEADS):
        lg = lg_ref[hd]
        inv_rho = jnp.exp(back * lg)
        consts.append((NORM_EPS * inv_rho * inv_rho,
                       jnp.exp(jnp.full((1, RET_DV), float(c), F32) * lg)))

    probs = {}
    for ci in range(RET_ROWS // c):
        rows = slice(ci * c, (ci + 1) * c)
        for hd in range(RET_HEADS):
            qk_cols = slice(hd * RET_DK, (hd + 1) * RET_DK)
            probs[ci, hd] = jnp.where(lower, _dot_nt(q_ref[rows, qk_cols], k_ref[rows, qk_cols]), 0.0).astype(BF16)

    for ci in range(RET_ROWS // c):
        rows = slice(ci * c, (ci + 1) * c)
        for hd in range(RET_HEADS):
            eps_rows, chunk_decay = consts[hd]
            qk_cols = slice(hd * RET_DK, (hd + 1) * RET_DK)
            v_cols = slice(hd * RET_DV, (hd + 1) * RET_DV)
            qc = q_ref[rows, qk_cols]
            kc = k_ref[rows, qk_cols]
            vc = v_ref[rows, v_cols]
            st = state_ref[hd] * chunk_decay
            p = probs[ci, hd]
            o = _dot(qc, st.astype(BF16)) + _dot(p, vc)
            state_ref[hd] = st + _dot_tn(kc, vc)
            y = o * lax.rsqrt(jnp.mean(o * o, axis=-1, keepdims=True) + eps_rows)
            o_ref[rows, v_cols] = y.astype(BF16) * gs_ref[rows, v_cols]


def _retention(log_gamma, q, k, v, gs, batch, seq):
    q3 = q.reshape(batch, seq, RET_QK)
    k3 = k.reshape(batch, seq, RET_QK)
    v3 = v.reshape(batch, seq, RET_V)
    g3 = gs.reshape(batch, seq, RET_V)
    qk_spec = pl.BlockSpec((None, RET_ROWS, RET_QK), lambda b, s: (b, s, 0))
    v_spec = pl.BlockSpec((None, RET_ROWS, RET_V), lambda b, s: (b, s, 0))
    out = pl.pallas_call(
        _retention_kernel,
        grid=(batch, seq // RET_ROWS),
        in_specs=[pl.BlockSpec(memory_space=pltpu.SMEM), qk_spec, qk_spec, v_spec, v_spec],
        out_specs=v_spec,
        out_shape=jax.ShapeDtypeStruct((batch, seq, RET_V), BF16),
        scratch_shapes=[pltpu.VMEM((RET_HEADS, RET_DK, RET_DV), F32)],
        compiler_params=_cparams("parallel", "arbitrary"),
        name="retention",
    )(log_gamma, q3, k3, v3, g3)
    return out.reshape(batch * seq, RET_V)


def _mix_ffn_kernel(y_ref, wm_ref, h_ref, g_ref, wi_ref, wo_ref, o_ref, act_ref):
    x = h_ref[...] + _dot(y_ref[...], wm_ref[...])
    hn = (_rms(x) * g_ref[...]).astype(BF16)
    for f in range(FFN_HIDDEN // FFN_CHUNK):
        lo = f * FFN_CHUNK
        a = _dot(hn, wi_ref[:, lo:lo + FFN_CHUNK])
        b = _dot(hn, wi_ref[:, FFN_HIDDEN + lo:FFN_HIDDEN + lo + FFN_CHUNK])
        act_ref[:, lo:lo + FFN_CHUNK] = (a * jax.nn.sigmoid(a) * b).astype(BF16)
    o_ref[...] = x + _dot(act_ref[...], wo_ref[...])


def _mix_ffn(y, w_mix, h, g, w_in, w_out):
    t, kdim = y.shape
    return pl.pallas_call(
        _mix_ffn_kernel,
        grid=(t // ROW_TILE,),
        in_specs=[_rows(kdim), _resident((kdim, D_MODEL)), _rows(D_MODEL), _resident((1, D_MODEL)),
                  _resident((D_MODEL, 2 * FFN_HIDDEN)), _resident((FFN_HIDDEN, D_MODEL))],
        out_specs=_rows(D_MODEL),
        out_shape=jax.ShapeDtypeStruct((t, D_MODEL), F32),
        scratch_shapes=[pltpu.VMEM((ROW_TILE, FFN_HIDDEN), BF16)],
        compiler_params=_cparams("parallel"),
        name="mix_ffn",
    )(y, w_mix, h, g, w_in, w_out)


def _kv_kernel(h_ref, g_ref, wd_ref, cg_ref, wuk_ref, wuv_ref, gn_ref, gr_ref,
               cm_ref, sm_ref, k_ref, v_ref):
    hn = (_rms(h_ref[...]) * g_ref[...]).astype(BF16)
    ckr = _dot(hn, wd_ref[...])
    c = (_rms(ckr[:, :KV_LORA]) * cg_ref[...]).astype(BF16)
    kr = ckr[:, KV_LORA:]
    kr_ss = jnp.sum(kr * kr, axis=-1, keepdims=True)
    krg = kr * gr_ref[...]
    sm = sm_ref[...]
    sa = jnp.where(_lane_mask(0, ROPE_HALF), sm, 0.0)
    sb = jnp.where(_lane_mask(ROPE_HALF, QK_ROPE), sm, 0.0)
    rot_a = krg * cm_ref[...] + pltpu.roll(krg, LANES - ROPE_HALF, 1) * sa + pltpu.roll(krg, ROPE_HALF, 1) * sb
    rot_b = pltpu.roll(rot_a, QK_ROPE, 1)
    gn = gn_ref[...]
    for hd in range(MLA_HEADS):
        kn = _dot(c, wuk_ref[:, hd * QK_NOPE:(hd + 1) * QK_NOPE])
        r = lax.rsqrt((jnp.sum(kn * kn, axis=-1, keepdims=True) + kr_ss) * (1.0 / QK_HEAD) + NORM_EPS)
        k_ref[:, hd * HEAD_PAD:hd * HEAD_PAD + LANES] = (kn * r * gn).astype(BF16)
        rot = rot_a if hd % 2 == 0 else rot_b
        k_ref[:, hd * HEAD_PAD + LANES:(hd + 1) * HEAD_PAD] = (rot * r).astype(BF16)
    v_ref[...] = _dot(c, wuv_ref[...]).astype(BF16)


def _shared_kv(h, g, wd, cg, wuk, wuv, gn, gr, cm, sm):
    t = h.shape[0]
    rows = lambda width: _rows(width, PROJ_TILE)
    return pl.pallas_call(
        _kv_kernel,
        grid=(t // PROJ_TILE,),
        in_specs=[rows(D_MODEL), _resident((1, D_MODEL)), _resident((D_MODEL, KV_LORA + LANES)),
                  _resident((1, KV_LORA)), _resident((KV_LORA, MLA_HEADS * QK_NOPE)),
                  _resident((KV_LORA, MLA_HEADS * V_HEAD)), _resident((1, LANES)), _resident((1, LANES)),
                  rows(LANES), rows(LANES)],
        out_specs=[rows(MLA_HEADS * HEAD_PAD), rows(MLA_HEADS * V_HEAD)],
        out_shape=[jax.ShapeDtypeStruct((t, MLA_HEADS * HEAD_PAD), BF16),
                   jax.ShapeDtypeStruct((t, MLA_HEADS * V_HEAD), BF16)],
        compiler_params=_cparams("parallel"),
        name="shared_kv",
    )(h, g, wd, cg, wuk, wuv, gn, gr, cm, sm)


def _q_proj_kernel(h_ref, g_ref, wd_ref, lg_ref, wu_ref, gn_ref, g2_ref, g3_ref, cm_ref, sm_ref, q_ref):
    hn = (_rms(h_ref[...]) * g_ref[...]).astype(BF16)
    cq = (_rms(_dot(hn, wd_ref[...])) * lg_ref[...]).astype(BF16)
    gn = gn_ref[...]
    gc = g2_ref[...] * cm_ref[...]
    gs = g3_ref[...] * sm_ref[...]
    head_a = _lane_mask(0, QK_ROPE)
    for pair in range(MLA_HEADS // 2):
        qh = _dot(cq, wu_ref[:, pair * 2 * HEAD_PAD:(pair + 1) * 2 * HEAD_PAD])
        na, nb = qh[:, :LANES], qh[:, LANES:2 * LANES]
        xr, xp = qh[:, 2 * LANES:3 * LANES], qh[:, 3 * LANES:]
        sq = xr * xr
        ss_a = jnp.sum(na * na + jnp.where(head_a, sq, 0.0), axis=-1, keepdims=True)
        ss_b = jnp.sum(nb * nb + jnp.where(head_a, 0.0, sq), axis=-1, keepdims=True)
        r_a = lax.rsqrt(ss_a * (1.0 / QK_HEAD) + NORM_EPS) * ATTN_SCALE
        r_b = lax.rsqrt(ss_b * (1.0 / QK_HEAD) + NORM_EPS) * ATTN_SCALE
        rot = ((xr * gc + xp * gs) * jnp.where(head_a, r_a, r_b)).astype(BF16)
        lo = pair * 2 * HEAD_PAD
        q_ref[:, lo:lo + LANES] = (na * r_a * gn).astype(BF16)
        q_ref[:, lo + LANES:lo + HEAD_PAD] = rot
        q_ref[:, lo + HEAD_PAD:lo + HEAD_PAD + LANES] = (nb * r_b * gn).astype(BF16)
        q_ref[:, lo + HEAD_PAD + LANES:lo + 2 * HEAD_PAD] = rot


def _q_proj(h, g, wd, lg, wu, gn, g2, g3, cm, sm):
    t = h.shape[0]
    rows = lambda width: _rows(width, PROJ_TILE)
    return pl.pallas_call(
        _q_proj_kernel,
        grid=(t // PROJ_TILE,),
        in_specs=[rows(D_MODEL), _resident((1, D_MODEL)), _resident((D_MODEL, Q_LORA)),
                  _resident((1, Q_LORA)), _resident((Q_LORA, MLA_HEADS * HEAD_PAD)),
                  _resident((1, LANES)), _resident((1, LANES)), _resident((1, LANES)),
                  rows(LANES), rows(LANES)],
        out_specs=rows(MLA_HEADS * HEAD_PAD),
        out_shape=jax.ShapeDtypeStruct((t, MLA_HEADS * HEAD_PAD), BF16),
        compiler_params=_cparams("parallel"),
        name="q_proj",
    )(h, g, wd, lg, wu, gn, g2, g3, cm, sm)


def _pair_up_proj(w):
    w = w.reshape(Q_LORA, MLA_HEADS // 2, 2, QK_HEAD)
    nope = w[..., :QK_NOPE]
    x1 = w[..., QK_NOPE:QK_NOPE + ROPE_HALF]
    x2 = w[..., QK_NOPE + ROPE_HALF:]
    a, b = 0, 1
    slab = jnp.concatenate(
        [nope[:, :, a], nope[:, :, b],
         x1[:, :, a], x2[:, :, a], x1[:, :, b], x2[:, :, b],
         x2[:, :, a], x1[:, :, a], x2[:, :, b], x1[:, :, b]], axis=-1)
    return slab.reshape(Q_LORA, MLA_HEADS * HEAD_PAD)


def _attn_kernel(q_ref, k_ref, v_ref, o_ref, vaug_ref):
    seq = k_ref.shape[0]
    tq = ATTN_TQ
    vaug_ref[:, :V_HEAD] = v_ref[...]
    vaug_ref[:, V_HEAD:] = jnp.ones((seq, HEAD_PAD - V_HEAD), BF16)
    row = lax.broadcasted_iota(jnp.int32, (tq, tq), 0)
    col = lax.broadcasted_iota(jnp.int32, (tq, tq), 1)
    causal = col <= row
    for qi in reversed(range(seq // tq)):
        n0 = qi * tq
        q = q_ref[n0:n0 + tq, :]
        s_diag = jnp.where(causal, _dot_nt(q, k_ref[n0:n0 + tq, :]), -jnp.inf)
        m = jnp.max(s_diag, axis=-1, keepdims=True)
        if qi > 0:
            s_off = _dot_nt(q, k_ref[:n0, :])
            m = jnp.maximum(m, jnp.max(s_off, axis=-1, keepdims=True))
            p = jnp.concatenate([jnp.exp2(s_off - m), jnp.exp2(s_diag - m)], axis=1)
        else:
            p = jnp.exp2(s_diag - m)
        oa = _dot(p.astype(BF16), vaug_ref[:n0 + tq, :])
        o_ref[n0:n0 + tq, :] = (oa[:, :V_HEAD] / oa[:, V_HEAD:]).astype(BF16)


def _attention(q, k, v, batch, seq):
    q3 = q.reshape(batch, seq, MLA_HEADS * HEAD_PAD)
    k3 = k.reshape(batch, seq, MLA_HEADS * HEAD_PAD)
    v3 = v.reshape(batch, seq, MLA_HEADS * V_HEAD)
    qk_spec = pl.BlockSpec((None, seq, HEAD_PAD), lambda b, h: (b, 0, h))
    v_spec = pl.BlockSpec((None, seq, V_HEAD), lambda b, h: (b, 0, h))
    out = pl.pallas_call(
        _attn_kernel,
        grid=(batch, MLA_HEADS),
        in_specs=[qk_spec, qk_spec, v_spec],
        out_specs=v_spec,
        out_shape=jax.ShapeDtypeStruct((batch, seq, MLA_HEADS * V_HEAD), BF16),
        scratch_shapes=[pltpu.VMEM((seq, HEAD_PAD), BF16)],
        compiler_params=_cparams("parallel", "parallel"),
        name="attention",
    )(q3, k3, v3)
    return out.reshape(batch * seq, MLA_HEADS * V_HEAD)


def _pad_lanes(a, width):
    return jnp.pad(a, [(0, 0)] * (a.ndim - 1) + [(0, width - a.shape[-1])])


def kernel(x, positions, attn_norm_g, ffn_norm_g, ffn_w_in, ffn_w_out, ret_w_in, ret_w_out,
           kv_norm_g, w_dkv, ckv_norm_g, w_uk, w_uv, k_norm_g,
           mla_w_dq, q_lora_norm_g, mla_w_uq, q_norm_g, mla_w_o):
    batch, seq, _ = x.shape
    t = batch * seq
    row = lambda a: a.reshape(1, -1)

    cr, sr, cm, sm = _rope_tables(positions)
    log_gamma = jnp.log(1.0 - 2.0 ** (-5.0 - jnp.arange(RET_HEADS, dtype=F32)))

    h = x.reshape(t, D_MODEL)
    for l in range(DEPTH):
        if l < N_RET_LAYERS:
            q, k, v, gs = _ret_proj(log_gamma, h, row(attn_norm_g[l]), ret_w_in[l].astype(BF16), cr, sr)
            y = _retention(log_gamma, q, k, v, gs, batch, seq)
            w_mix = ret_w_out[l]
        else:
            j = l - N_RET_LAYERS
            gx1 = q_norm_g[j][QK_NOPE:QK_NOPE + ROPE_HALF]
            gx2 = q_norm_g[j][QK_NOPE + ROPE_HALF:]
            q = _q_proj(h, row(attn_norm_g[l]), mla_w_dq[j].astype(BF16), row(q_lora_norm_g[j]),
                        _pair_up_proj(mla_w_uq[j]).astype(BF16), row(q_norm_g[j][:QK_NOPE]),
                        row(jnp.concatenate([gx1, gx2, gx1, gx2])), row(jnp.concatenate([gx2, gx1, gx2, gx1])),
                        cm, sm)
            y = _attention(q, shared_k, shared_v, batch, seq)
            w_mix = mla_w_o[j]
        h = _mix_ffn(y, w_mix.astype(BF16), h, row(ffn_norm_g[l]),
                     ffn_w_in[l].astype(BF16), ffn_w_out[l].astype(BF16))
        if l == N_RET_LAYERS - 1:
            shared_k, shared_v = _shared_kv(
                h, row(kv_norm_g), _pad_lanes(w_dkv, KV_LORA + LANES).astype(BF16), row(ckv_norm_g),
                w_uk.astype(BF16), w_uv.astype(BF16),
                row(k_norm_g[:QK_NOPE]), row(_pad_lanes(k_norm_g[QK_NOPE:], LANES)), cm, sm)
    return h.reshape(batch, seq, D_MODEL)
```
